```python
import math, functools
import jax, jax.numpy as jnp
from jax import lax
import numpy as np

D_MODEL = 1024
BATCH = 2
SEQ = 8192
DEPTH = 1
DEC_BATCH = 128
DEC_SEQ = 4
PAST_LEN = 8192
PAGE_SIZE = 128

N_HEADS = 8
HEAD_DIM = 128
D_ATTN = N_HEADS * HEAD_DIM
BLOCK_SIZE = 256
TOP_K = 3
Q_CHUNK = 32
ROPE_THETA = 10000.0
POOL_WINDOWS = (2, 4, 8, 16)
N_POOL_GROUPS = len(POOL_WINDOWS)
POOL_GROUP_DIM = D_MODEL // 8
D_POOL = N_POOL_GROUPS * POOL_GROUP_DIM
POOL_STATE_LEN = max(POOL_WINDOWS) - 1
D_FF = -(-8 * D_MODEL // (3 * 256)) * 256
ALPHA = (2.0 * DEPTH) ** 0.25
BETA = (8.0 * DEPTH) ** -0.25
LN_EPS = 1e-5
NEG_INF = -1e30
N_IN = D_POOL + 3 * D_ATTN + 2 * D_MODEL
SPLITS = (D_POOL, D_POOL + D_ATTN, D_POOL + 2 * D_ATTN, D_POOL + 3 * D_ATTN, D_POOL + 3 * D_ATTN + D_MODEL)

kernel_name = 'moba_pool_hybrid_step'


def layer_norm(x, g, b):
    xf = x.astype(jnp.float32)
    mu = xf.mean(-1, keepdims=True)
    var = jnp.square(xf - mu).mean(-1, keepdims=True)
    return ((xf - mu) * lax.rsqrt(var + LN_EPS) * g + b).astype(x.dtype)


def rope(x, pos):
    half = HEAD_DIM // 2
    inv = ROPE_THETA ** (-jnp.arange(half, dtype=jnp.float32) / half)
    ang = pos.astype(jnp.float32)[:, None] * inv
    cos = jnp.cos(ang)[None, :, None, :]
    sin = jnp.sin(ang)[None, :, None, :]
    xf = x.astype(jnp.float32)
    x1, x2 = xf[..., :half], xf[..., half:]
    return jnp.concatenate([x1 * cos - x2 * sin, x2 * cos + x1 * sin], -1).astype(x.dtype)


def pool_mix(u, prev, pos, w_pool, pool_scale):
    B, T, _ = u.shape
    P = POOL_STATE_LEN
    ext_raw = jnp.concatenate([prev, u], axis=1)
    ext = ext_raw.astype(jnp.float32)
    cs = jnp.pad(jnp.cumsum(ext, axis=1), ((0, 0), (1, 0), (0, 0)))
    groups = []
    for g, w in enumerate(POOL_WINDOWS):
        sl = slice(g * POOL_GROUP_DIM, (g + 1) * POOL_GROUP_DIM)
        win = cs[:, P + 1:P + 1 + T, sl] - cs[:, P + 1 - w:P + 1 - w + T, sl]
        cnt = jnp.minimum(pos + 1, w).astype(jnp.float32)[None, :, None]
        groups.append(win / cnt - ext[:, P:, sl])
    d = jnp.stack(groups, axis=2).astype(u.dtype)
    y = jnp.einsum('btgc,gcd->btgd', d, w_pool).reshape(B, T, D_POOL) * pool_scale
    return y, ext_raw[:, T:]


def moba_core(q, k_sel, v_sel, sel_mask, k_own, v_own, own_mask):
    scale = HEAD_DIM ** -0.5
    s_sel = jnp.einsum('bhqd,bhqsd->bhqs', q, k_sel).astype(jnp.float32) * scale
    s_sel = jnp.where(sel_mask, s_sel, NEG_INF)
    s_own = jnp.einsum('bhqd,bhkd->bhqk', q, k_own).astype(jnp.float32) * scale
    s_own = jnp.where(own_mask, s_own, NEG_INF)
    n_sel = s_sel.shape[-1]
    p = jax.nn.softmax(jnp.concatenate([s_sel, s_own], axis=-1), axis=-1).astype(v_own.dtype)
    return (jnp.einsum('bhqs,bhqsd->bhqd', p[..., :n_sel], v_sel)
            + jnp.einsum('bhqk,bhkd->bhqd', p[..., n_sel:], v_own))


def moba_prompt(q, k, v):
    B, T = q.shape[:2]
    n_blk = -(-T // BLOCK_SIZE)
    pad = n_blk * BLOCK_SIZE - T

    def to_blocks(a):
        a = jnp.pad(a, ((0, 0), (0, pad), (0, 0), (0, 0)))
        return a.reshape(B, n_blk, BLOCK_SIZE, N_HEADS, HEAD_DIM).transpose(0, 3, 1, 2, 4)

    kb, vb = to_blocks(k), to_blocks(v)
    k_mean = kb.mean(axis=3)
    qh = q.transpose(0, 2, 1, 3)
    k_sel = min(TOP_K, n_blk)
    bi = jnp.arange(B)[:, None, None, None]
    hi = jnp.arange(N_HEADS)[None, :, None, None]
    blk_ids = jnp.arange(n_blk)

    def chunk(c):
        start = c * Q_CHUNK
        qc = lax.dynamic_slice_in_dim(qh, start, Q_CHUNK, axis=2)
        qpos = start + jnp.arange(Q_CHUNK)
        blk = start // BLOCK_SIZE
        gate = jnp.einsum('bhqd,bhnd->bhqn', qc, k_mean).astype(jnp.float32)
        gate = jnp.where(blk_ids < blk, gate, NEG_INF)
        _, idx = lax.top_k(gate, k_sel)
        k_g = kb[bi, hi, idx].reshape(B, N_HEADS, Q_CHUNK, k_sel * BLOCK_SIZE, HEAD_DIM)
        v_g = vb[bi, hi, idx].reshape(B, N_HEADS, Q_CHUNK, k_sel * BLOCK_SIZE, HEAD_DIM)
        sel_mask = jnp.repeat(jnp.arange(k_sel) < blk, BLOCK_SIZE)
        k_own = lax.dynamic_index_in_dim(kb, blk, axis=2, keepdims=False)
        v_own = lax.dynamic_index_in_dim(vb, blk, axis=2, keepdims=False)
        own_mask = (blk * BLOCK_SIZE + jnp.arange(BLOCK_SIZE))[None, :] <= qpos[:, None]
        return moba_core(qc, k_g, v_g, sel_mask, k_own, v_own, own_mask)

    out = lax.map(chunk, jnp.arange(T // Q_CHUNK))
    return out.transpose(1, 0, 3, 2, 4).reshape(B, T, D_ATTN)


def moba_sample(q, k, v, cache_k, cache_v, page_table):
    DB, T = q.shape[:2]
    n_full = PAST_LEN // BLOCK_SIZE
    ppb = BLOCK_SIZE // PAGE_SIZE
    own_past = PAST_LEN - n_full * BLOCK_SIZE
    qh = q.transpose(0, 2, 1, 3)
    if n_full > 0:
        k_sel = min(TOP_K, n_full)
        k_mean = cache_k[page_table[:, :n_full * ppb]].reshape(
            DB, n_full, BLOCK_SIZE, N_HEADS, HEAD_DIM).mean(axis=2)
        gate = jnp.einsum('bhqd,bnhd->bhqn', qh, k_mean).astype(jnp.float32)
        _, idx = lax.top_k(gate, k_sel)
        bi = jnp.arange(DB)[:, None, None, None, None]
        hi = jnp.arange(N_HEADS)[None, :, None, None, None]
        phys = page_table[bi, idx[..., None] * ppb + jnp.arange(ppb)]
        k_g = cache_k[phys, :, hi].reshape(DB, N_HEADS, T, k_sel * BLOCK_SIZE, HEAD_DIM)
        v_g = cache_v[phys, :, hi].reshape(DB, N_HEADS, T, k_sel * BLOCK_SIZE, HEAD_DIM)
    else:
        k_g = jnp.zeros((DB, N_HEADS, T, 0, HEAD_DIM), q.dtype)
        v_g = jnp.zeros((DB, N_HEADS, T, 0, HEAD_DIM), q.dtype)
    sel_mask = jnp.ones((k_g.shape[3],), dtype=bool)
    own_pages = page_table[:, n_full * ppb:]
    k_op = cache_k[own_pages].reshape(DB, own_past, N_HEADS, HEAD_DIM)
    v_op = cache_v[own_pages].reshape(DB, own_past, N_HEADS, HEAD_DIM)
    k_own = jnp.concatenate([k_op, k], axis=1).transpose(0, 2, 1, 3)
    v_own = jnp.concatenate([v_op, v], axis=1).transpose(0, 2, 1, 3)
    own_mask = jnp.arange(own_past + T)[None, :] <= (own_past + jnp.arange(T))[:, None]
    out = moba_core(qh, k_g, v_g, sel_mask, k_own, v_own, own_mask)
    return out.transpose(0, 2, 1, 3).reshape(DB, T, D_ATTN)


def hybrid_layer(x, pos0, pool_prev, attend, w_in, w_pool, pool_scale, w_up_pool, w_up_attn, w_out,
                 ln1_g, ln1_b, w_gate, w_up, w_down, ln2_g, ln2_b):
    B, T, _ = x.shape
    pos = pos0 + jnp.arange(T)
    h = x @ w_in
    u, q, k, v, ga, gb = jnp.split(h, SPLITS, axis=-1)
    q = rope(q.reshape(B, T, N_HEADS, HEAD_DIM), pos)
    k = rope(k.reshape(B, T, N_HEADS, HEAD_DIM), pos)
    v = v.reshape(B, T, N_HEADS, HEAD_DIM)
    pool_out, pool_state = pool_mix(u, pool_prev, pos, w_pool, pool_scale)
    attn_out = attend(q, k, v)
    mix = jax.nn.sigmoid(ga) * (pool_out @ w_up_pool) + jax.nn.sigmoid(gb) * (attn_out @ w_up_attn)
    x1 = layer_norm(ALPHA * x + mix @ w_out, ln1_g, ln1_b)
    ffn = (jax.nn.silu(x1 @ w_gate) * (x1 @ w_up)) @ w_down
    y = layer_norm(ALPHA * x1 + ffn, ln2_g, ln2_b)
    return y, k, v, pool_state


def setup_inputs(seed: int = 0) -> dict:
    key = jax.random.key(seed)
    ks = jax.random.split(key, 20)
    n_pages = PAST_LEN // PAGE_SIZE
    n_used = DEC_BATCH * n_pages
    n_phys = n_used + n_used // 4
    nrm = jax.random.normal
    f32 = jnp.float32
    x_prompt = nrm(ks[0], (BATCH, SEQ, D_MODEL), f32)
    x_sample = nrm(ks[1], (DEC_BATCH, DEC_SEQ, D_MODEL), f32)
    cache_k = nrm(ks[2], (DEPTH, n_phys, PAGE_SIZE, N_HEADS, HEAD_DIM), f32)
    cache_v = nrm(ks[3], (DEPTH, n_phys, PAGE_SIZE, N_HEADS, HEAD_DIM), f32)
    state_pool = nrm(ks[4], (DEPTH, DEC_BATCH, POOL_STATE_LEN, D_POOL), f32)
    page_table = jax.random.permutation(ks[5], n_phys)[:n_used].reshape(DEC_BATCH, n_pages).astype(jnp.int32)
    col_scale = jnp.concatenate([jnp.ones((D_POOL + 2 * D_ATTN,), f32), jnp.full((D_ATTN,), BETA, f32),
                                 jnp.ones((2 * D_MODEL,), f32)])
    w_in = nrm(ks[6], (DEPTH, D_MODEL, N_IN), f32) * D_MODEL ** -0.5 * col_scale
    w_pool = nrm(ks[7], (DEPTH, N_POOL_GROUPS, POOL_GROUP_DIM, POOL_GROUP_DIM), f32) * POOL_GROUP_DIM ** -0.5
    pool_scale = 1.0 + 0.1 * nrm(ks[8], (DEPTH, D_POOL), f32)
    w_up_pool = nrm(ks[9], (DEPTH, D_POOL, D_MODEL), f32) * BETA * D_POOL ** -0.5
    w_up_attn = nrm(ks[10], (DEPTH, D_ATTN, D_MODEL), f32) * BETA * D_ATTN ** -0.5
    w_out = nrm(ks[11], (DEPTH, D_MODEL, D_MODEL), f32) * BETA * D_MODEL ** -0.5
    ln1_g = 1.0 + 0.1 * nrm(ks[12], (DEPTH, D_MODEL), f32)
    ln1_b = 0.02 * nrm(ks[13], (DEPTH, D_MODEL), f32)
    w_gate = nrm(ks[14], (DEPTH, D_MODEL, D_FF), f32) * BETA * D_MODEL ** -0.5
    w_up = nrm(ks[15], (DEPTH, D_MODEL, D_FF), f32) * BETA * D_MODEL ** -0.5
    w_down = nrm(ks[16], (DEPTH, D_FF, D_MODEL), f32) * BETA * D_FF ** -0.5
    ln2_g = 1.0 + 0.1 * nrm(ks[17], (DEPTH, D_MODEL), f32)
    ln2_b = 0.02 * nrm(ks[18], (DEPTH, D_MODEL), f32)
    return {'x_prompt': x_prompt, 'x_sample': x_sample, 'cache_k': cache_k, 'cache_v': cache_v,
            'state_pool': state_pool, 'page_table': page_table, 'w_in': w_in, 'w_pool': w_pool,
            'pool_scale': pool_scale, 'w_up_pool': w_up_pool, 'w_up_attn': w_up_attn, 'w_out': w_out,
            'ln1_g': ln1_g, 'ln1_b': ln1_b, 'w_gate': w_gate, 'w_up': w_up, 'w_down': w_down,
            'ln2_g': ln2_g, 'ln2_b': ln2_b}


def reference(x_prompt, x_sample, cache_k, cache_v, state_pool, page_table, w_in, w_pool, pool_scale,
              w_up_pool, w_up_attn, w_out, ln1_g, ln1_b, w_gate, w_up, w_down, ln2_g, ln2_b):
    yp, ys = x_prompt, x_sample
    kp_l, vp_l, pp_l, ks_l, vs_l, ps_l = [], [], [], [], [], []
    for l in range(DEPTH):
        params = (w_in[l], w_pool[l], pool_scale[l], w_up_pool[l], w_up_attn[l], w_out[l],
                  ln1_g[l], ln1_b[l], w_gate[l], w_up[l], w_down[l], ln2_g[l], ln2_b[l])
        zero_prev = jnp.zeros((yp.shape[0], POOL_STATE_LEN, D_POOL), yp.dtype)
        yp, kp, vp, pp = hybrid_layer(yp, 0, zero_prev, moba_prompt, *params)
        attend_s = functools.partial(moba_sample, cache_k=cache_k[l], cache_v=cache_v[l], page_table=page_table)
        ys, k_s, v_s, p_s = hybrid_layer(ys, PAST_LEN, state_pool[l], attend_s, *params)
        kp_l.append(kp); vp_l.append(vp); pp_l.append(pp)
        ks_l.append(k_s); vs_l.append(v_s); ps_l.append(p_s)
    k_prompt = jnp.stack(kp_l)
    v_prompt = jnp.stack(vp_l)
    pool_prompt = jnp.stack(pp_l)
    k_sample = jnp.stack(ks_l)
    v_sample = jnp.stack(vs_l)
    pool_sample = jnp.stack(ps_l)
    return (yp, ys, k_prompt, v_prompt, pool_prompt, k_sample, v_sample, pool_sample)
```

```python
import functools

import jax
import jax.numpy as jnp
from jax import lax
from jax.experimental import pallas as pl
from jax.experimental.pallas import tpu as pltpu

F32 = jnp.float32
BF16 = jnp.bfloat16

N_HEADS = 8
HEAD_DIM = 128
D_ATTN = N_HEADS * HEAD_DIM
BLOCK_SIZE = 256
TOP_K = 3
PAGE_SIZE = 128
PAGES_PER_BLOCK = BLOCK_SIZE // PAGE_SIZE
ROPE_THETA = 10000.0
POOL_WINDOWS = (2, 4, 8, 16)
POOL_STATE_LEN = max(POOL_WINDOWS) - 1
POOL_HALO = 16
LN_EPS = 1e-5
NEG_INF = -1e30
SCALE = HEAD_DIM ** -0.5

ROW_TILE = 256
POOL_TILE = 512
PAGES_PER_STEP = 16
VMEM_LIMIT = 56 * 1024 * 1024

NT_DIMS = (((1,), (1,)), ((), ()))


def _resident(shape):
    return pl.BlockSpec(shape, lambda *_: (0,) * len(shape), pipeline_mode=pl.Buffered(1))


def _layer_norm(z, g, b):
    mu = jnp.mean(z, axis=-1, keepdims=True)
    zc = z - mu
    var = jnp.mean(zc * zc, axis=-1, keepdims=True)
    return zc * lax.rsqrt(var + LN_EPS) * g + b


def _proj_kernel(x_ref, w_ref, cos_ref, sin_ref, u_ref, q_ref, kf_ref, kb_ref, vf_ref, vb_ref,
                 ga_ref, gb_ref, *km_ref, d_pool, q_scale):
    xb = x_ref[...].astype(BF16)
    cos = cos_ref[...]
    sin = sin_ref[...]

    def proj(c0, width):
        return jnp.dot(xb, w_ref[:, c0:c0 + width], preferred_element_type=F32)

    def rope(h):
        return h * cos + pltpu.roll(h, HEAD_DIM // 2, 1) * sin

    u_ref[...] = proj(0, d_pool)
    half = D_ATTN // 2
    for c in range(2):
        acc = proj(d_pool + c * half, half)
        for hh in range(half // HEAD_DIM):
            sl = slice(c * half + hh * HEAD_DIM, c * half + (hh + 1) * HEAD_DIM)
            r = rope(acc[:, hh * HEAD_DIM:(hh + 1) * HEAD_DIM])
            q_ref[:, sl] = (r * q_scale).astype(q_ref.dtype)
    for c in range(2):
        acc = proj(d_pool + D_ATTN + c * half, half)
        for hh in range(half // HEAD_DIM):
            sl = slice(c * half + hh * HEAD_DIM, c * half + (hh + 1) * HEAD_DIM)
            r = rope(acc[:, hh * HEAD_DIM:(hh + 1) * HEAD_DIM])
            kf_ref[:, sl] = r
            kb_ref[:, sl] = r.astype(BF16)
            if km_ref:
                km_ref[0][0, :, sl] = jnp.sum(r, axis=0, keepdims=True) * (1.0 / r.shape[0])
    for c in range(2):
        acc = proj(d_pool + 2 * D_ATTN + c * half, half)
        vf_ref[:, c * half:(c + 1) * half] = acc
        vb_ref[:, c * half:(c + 1) * half] = acc.astype(BF16)
    d_model = ga_ref.shape[1]
    for c in range(2):
        hw = d_model // 2
        ga_ref[:, c * hw:(c + 1) * hw] = proj(d_pool + 3 * D_ATTN + c * hw, hw)
        gb_ref[:, c * hw:(c + 1) * hw] = proj(d_pool + 3 * D_ATTN + d_model + c * hw, hw)


def _proj(x, w_in_b, cos, sin, *, d_pool, q_dtype, q_scale, block_means):
    m, d_model = x.shape
    n_in = w_in_b.shape[1]
    tm = ROW_TILE
    n_tab = cos.shape[0] // tm
    row = lambda i: (i, 0)
    out_shape = [
        jax.ShapeDtypeStruct((m, d_pool), F32),
        jax.ShapeDtypeStruct((m, D_ATTN), q_dtype),
        jax.ShapeDtypeStruct((m, D_ATTN), F32),
        jax.ShapeDtypeStruct((m, D_ATTN), BF16),
        jax.ShapeDtypeStruct((m, D_ATTN), F32),
        jax.ShapeDtypeStruct((m, D_ATTN), BF16),
        jax.ShapeDtypeStruct((m, d_model), F32),
        jax.ShapeDtypeStruct((m, d_model), F32),
    ]
    out_specs = [
        pl.BlockSpec((tm, d_pool), row),
        pl.BlockSpec((tm, D_ATTN), row),
        pl.BlockSpec((tm, D_ATTN), row),
        pl.BlockSpec((tm, D_ATTN), row),
        pl.BlockSpec((tm, D_ATTN), row),
        pl.BlockSpec((tm, D_ATTN), row),
        pl.BlockSpec((tm, d_model), row),
        pl.BlockSpec((tm, d_model), row),
    ]
    if block_means:
        out_shape.append(jax.ShapeDtypeStruct((m // tm, 1, D_ATTN), F32))
        out_specs.append(pl.BlockSpec((1, 1, D_ATTN), lambda i: (i, 0, 0)))
    return pl.pallas_call(
        functools.partial(_proj_kernel, d_pool=d_pool, q_scale=q_scale),
        grid=(m // tm,),
        in_specs=[
            pl.BlockSpec((tm, d_model), row),
            _resident((d_model, n_in)),
            pl.BlockSpec((tm, HEAD_DIM), lambda i: (i % n_tab, 0)),
            pl.BlockSpec((tm, HEAD_DIM), lambda i: (i % n_tab, 0)),
        ],
        out_specs=out_specs,
        out_shape=out_shape,
        compiler_params=pltpu.CompilerParams(dimension_semantics=("parallel",), vmem_limit_bytes=VMEM_LIMIT),
        name="proj_rope",
    )(x, w_in_b, cos, sin)


def _rope_tables(pos):
    half = HEAD_DIM // 2
    inv = ROPE_THETA ** (-jnp.arange(half, dtype=F32) / half)
    ang = pos.astype(F32)[:, None] * inv
    cos, sin = jnp.cos(ang), jnp.sin(ang)
    return jnp.concatenate([cos, cos], -1), jnp.concatenate([-sin, sin], -1)


def _pool_prompt_kernel(u_ref, halo_ref, wp_ref, ps_ref, o_ref, ext_scr):
    i = pl.program_id(1)
    tt = u_ref.shape[1]
    gd = wp_ref.shape[1]
    ext_scr[0:POOL_HALO, :] = jnp.where(i == 0, 0.0, halo_ref[0])
    ext_scr[POOL_HALO:, :] = u_ref[0]
    pos = i * tt + lax.broadcasted_iota(jnp.int32, (tt, 1), 0)
    for g, w in enumerate(POOL_WINDOWS):
        cols = slice(g * gd, (g + 1) * gd)
        win = ext_scr[POOL_HALO:POOL_HALO + tt, cols]
        for back in range(1, w):
            win = win + ext_scr[POOL_HALO - back:POOL_HALO - back + tt, cols]
        cnt = jnp.minimum(pos + 1, w).astype(F32)
        d = win / cnt - ext_scr[POOL_HALO:POOL_HALO + tt, cols]
        y = jnp.dot(d.astype(BF16), wp_ref[g], preferred_element_type=F32) * ps_ref[:, cols]
        o_ref[0, :, cols] = y.astype(o_ref.dtype)


def _pool_prompt(u, w_pool_b, pool_scale):
    b, t, d_pool = u.shape
    tt = POOL_TILE
    halo_blocks = tt // POOL_HALO
    return pl.pallas_call(
        _pool_prompt_kernel,
        grid=(b, t // tt),
        in_specs=[
            pl.BlockSpec((1, tt, d_pool), lambda bi, i: (bi, i, 0)),
            pl.BlockSpec((1, POOL_HALO, d_pool), lambda bi, i: (bi, jnp.maximum(i * halo_blocks - 1, 0), 0)),
            _resident(w_pool_b.shape),
            _resident(pool_scale.shape),
        ],
        out_specs=pl.BlockSpec((1, tt, d_pool), lambda bi, i: (bi, i, 0)),
        out_shape=jax.ShapeDtypeStruct((b, t, d_pool), BF16),
        scratch_shapes=[pltpu.VMEM((POOL_HALO + tt, d_pool), F32)],
        compiler_params=pltpu.CompilerParams(dimension_semantics=("parallel", "parallel")),
        name="pool_prompt",
    )(u, u, w_pool_b, pool_scale)


def _pool_sample_kernel(st_ref, u_ref, wp_ref, ps_ref, o_ref):
    t_new = u_ref.shape[0]
    gd = wp_ref.shape[1]
    p = POOL_STATE_LEN

    def ext_row(j, cols):
        return st_ref[j, :, cols] if j < p else u_ref[j - p, :, cols]

    for g, w in enumerate(POOL_WINDOWS):
        cols = slice(g * gd, (g + 1) * gd)
        for t in range(t_new):
            win = ext_row(p + t, cols)
            for back in range(1, w):
                win = win + ext_row(p + t - back, cols)
            d = win / float(w) - ext_row(p + t, cols)
            y = jnp.dot(d.astype(BF16), wp_ref[g], preferred_element_type=F32) * ps_ref[:, cols]
            o_ref[t, :, cols] = y.astype(o_ref.dtype)


def _pool_sample(state_t, u_t, w_pool_b, pool_scale):
    t_new, db, d_pool = u_t.shape
    return pl.pallas_call(
        _pool_sample_kernel,
        grid=(1,),
        in_specs=[_resident(state_t.shape), _resident(u_t.shape), _resident(w_pool_b.shape),
                  _resident(pool_scale.shape)],
        out_specs=pl.BlockSpec((t_new, db, d_pool), lambda i: (0, 0, 0)),
        out_shape=jax.ShapeDtypeStruct((t_new, db, d_pool), BF16),
        name="pool_sample",
    )(state_t, u_t, w_pool_b, pool_scale)


def _select_blocks(g, n_valid, k_keep):
    nb = g.shape[0]
    n_iota = lax.broadcasted_iota(jnp.int32, g.shape, 0)
    valid = n_iota < n_valid
    g = jnp.where(valid, g, NEG_INF)
    rank = jnp.zeros(g.shape, jnp.int32)
    for m in range(nb):
        row = g[m:m + 1, :]
        ahead = (row > g) | ((row == g) & (n_iota > m))
        rank = rank + ahead.astype(jnp.int32)
    return rank, valid & (rank < k_keep)


def _moba_prompt_kernel(q_ref, k_ref, vt_ref, km_ref, o_ref, sel_scr, acc_scr):
    qi = pl.program_id(2)
    q = q_ref[0]
    km = km_ref[0]
    km_hi = km.astype(BF16)
    km_lo = (km - km_hi.astype(F32)).astype(BF16)
    gate = (lax.dot_general(km_hi, q, NT_DIMS, preferred_element_type=F32)
            + lax.dot_general(km_lo, q, NT_DIMS, preferred_element_type=F32))
    _, sel = _select_blocks(gate, qi, jnp.minimum(TOP_K, qi))
    sel_scr[...] = sel.astype(F32)

    def scores_t(j):
        kj = k_ref[0, pl.ds(pl.multiple_of(j * BLOCK_SIZE, BLOCK_SIZE), BLOCK_SIZE), :]
        return lax.dot_general(kj, q, NT_DIMS, preferred_element_type=F32)

    s = scores_t(qi)
    key_i = lax.broadcasted_iota(jnp.int32, s.shape, 0)
    qry_i = lax.broadcasted_iota(jnp.int32, s.shape, 1)
    s = jnp.where(key_i <= qry_i, s, NEG_INF)
    m0 = jnp.max(s, axis=0, keepdims=True)
    p = jnp.exp(s - m0)
    l0 = jnp.sum(p, axis=0, keepdims=True)
    acc_scr[...] = jnp.dot(vt_ref[0, 0, qi], p.astype(BF16), preferred_element_type=F32)

    def body(j, carry):
        m, l = carry
        s = scores_t(j)
        s = jnp.where(sel_scr[pl.ds(j, 1), :] > 0.5, s, NEG_INF)
        m_new = jnp.maximum(m, jnp.max(s, axis=0, keepdims=True))
        alpha = jnp.exp(m - m_new)
        p = jnp.exp(s - m_new)
        l = alpha * l + jnp.sum(p, axis=0, keepdims=True)
        acc_scr[...] = acc_scr[...] * alpha + jnp.dot(vt_ref[0, 0, j], p.astype(BF16),
                                                       preferred_element_type=F32)
        return m_new, l

    _, l = lax.fori_loop(0, qi, body, (m0, l0))
    o_ref[0] = (acc_scr[...] / l).T.astype(o_ref.dtype)


def _moba_prompt(q_b, k_b, v_b, kmean):
    b, t, _ = q_b.shape
    nb = t // BLOCK_SIZE
    v_t = v_b.reshape(b, nb, BLOCK_SIZE, N_HEADS, HEAD_DIM).transpose(0, 3, 1, 4, 2)
    return pl.pallas_call(
        _moba_prompt_kernel,
        grid=(b, N_HEADS, nb),
        in_specs=[
            pl.BlockSpec((1, BLOCK_SIZE, HEAD_DIM), lambda bi, h, i: (bi, i, h)),
            pl.BlockSpec((1, t, HEAD_DIM), lambda bi, h, i: (bi, 0, h)),
            pl.BlockSpec((1, 1, nb, HEAD_DIM, BLOCK_SIZE), lambda bi, h, i: (bi, h, 0, 0, 0)),
            pl.BlockSpec((1, nb, HEAD_DIM), lambda bi, h, i: (bi, 0, h)),
        ],
        out_specs=pl.BlockSpec((1, BLOCK_SIZE, HEAD_DIM), lambda bi, h, i: (bi, i, h)),
        out_shape=jax.ShapeDtypeStruct((b, t, D_ATTN), BF16),
        scratch_shapes=[pltpu.VMEM((nb, BLOCK_SIZE), F32), pltpu.VMEM((HEAD_DIM, BLOCK_SIZE), F32)],
        compiler_params=pltpu.CompilerParams(dimension_semantics=("parallel", "parallel", "arbitrary")),
        name="moba_prompt",
    )(q_b, k_b, v_t, kmean)


def _cache_block_mean_kernel(pt_ref, *refs):
    pages, o_ref = refs[:-1], refs[-1]
    for n in range(len(pages) // PAGES_PER_BLOCK):
        tot = jnp.sum(pages[PAGES_PER_BLOCK * n][0], axis=0, keepdims=True)
        for i in range(1, PAGES_PER_BLOCK):
            tot = tot + jnp.sum(pages[PAGES_PER_BLOCK * n + i][0], axis=0, keepdims=True)
        o_ref[0, n:n + 1, :] = tot * (1.0 / BLOCK_SIZE)


def _cache_block_mean(cache_pages, pt_flat, page0, db, n_pages):
    steps = n_pages // PAGES_PER_STEP
    blocks_per_step = PAGES_PER_STEP // PAGES_PER_BLOCK

    def page_map(bi, c, pt, *, i):
        return (page0 + pt[bi * n_pages + c * PAGES_PER_STEP + i], 0, 0)

    return pl.pallas_call(
        _cache_block_mean_kernel,
        grid_spec=pltpu.PrefetchScalarGridSpec(
            num_scalar_prefetch=1,
            grid=(db, steps),
            in_specs=[pl.BlockSpec((1, PAGE_SIZE, D_ATTN), functools.partial(page_map, i=i))
                      for i in range(PAGES_PER_STEP)],
            out_specs=pl.BlockSpec((1, blocks_per_step, D_ATTN), lambda bi, c, pt: (bi, c, 0)),
        ),
        out_shape=jax.ShapeDtypeStruct((db, n_pages // PAGES_PER_BLOCK, D_ATTN), F32),
        compiler_params=pltpu.CompilerParams(dimension_semantics=("parallel", "arbitrary"),
                                             vmem_limit_bytes=VMEM_LIMIT),
        name="cache_block_mean",
    )(pt_flat, *([cache_pages] * PAGES_PER_STEP))


def _sample_topk_kernel(q_ref, km_ref, idx_ref):
    q = q_ref[0]
    km = km_ref[0]
    nb = km.shape[0]
    lane = lax.broadcasted_iota(jnp.int32, (nb, 128), 1)
    blk = lax.broadcasted_iota(jnp.int32, (nb, 128), 0).astype(F32)
    idx_ref[...] = jnp.zeros(idx_ref.shape, jnp.int32)
    for t in range(q.shape[0]):
        prod = km * q[t:t + 1, :]
        gate = jnp.zeros((nb, 128), F32)
        for h in range(N_HEADS):
            s = jnp.sum(prod[:, h * HEAD_DIM:(h + 1) * HEAD_DIM], axis=1, keepdims=True)
            gate = jnp.where(lane == h, s, gate)
        rank, _ = _select_blocks(gate, nb, TOP_K)
        for k in range(TOP_K):
            idx_ref[0, TOP_K * t + k:TOP_K * t + k + 1, :] = jnp.sum(
                jnp.where(rank == k, blk, 0.0), axis=0, keepdims=True).astype(jnp.int32)


def _sample_topk(q_s, kmean):
    db, t_new, _ = q_s.shape
    nb = kmean.shape[1]
    rows = -(-t_new * TOP_K // 8) * 8
    out = pl.pallas_call(
        _sample_topk_kernel,
        grid=(db,),
        in_specs=[pl.BlockSpec((1, t_new, D_ATTN), lambda bi: (bi, 0, 0)),
                  pl.BlockSpec((1, nb, D_ATTN), lambda bi: (bi, 0, 0))],
        out_specs=pl.BlockSpec((1, rows, 128), lambda bi: (bi, 0, 0)),
        out_shape=jax.ShapeDtypeStruct((db, rows, 128), jnp.int32),
        compiler_params=pltpu.CompilerParams(dimension_semantics=("parallel",)),
        name="sample_topk",
    )(q_s, kmean)
    return out[:, :t_new * TOP_K, :N_HEADS].reshape(db, t_new, TOP_K, N_HEADS)


def _sample_attn_kernel(pt_ref, idx_ref, q_ref, kn_ref, vn_ref, *refs, t_new, n_sel):
    k_pages, v_pages, o_ref = refs[:t_new * n_sel], refs[t_new * n_sel:2 * t_new * n_sel], refs[-1]
    q = q_ref[0]
    kn = kn_ref[0]
    vn = vn_ref[0]
    own_i = lax.broadcasted_iota(jnp.int32, (t_new, 1), 0)
    for t in range(t_new):
        qt = q[t:t + 1, :]
        s_sel = [jnp.sum(k_pages[t * n_sel + p][0] * qt, axis=1, keepdims=True) * SCALE for p in range(n_sel)]
        s_own = jnp.sum(kn * qt, axis=1, keepdims=True) * SCALE
        s_own = jnp.where(own_i <= t, s_own, NEG_INF)
        m = jnp.max(s_own, axis=0, keepdims=True)
        for s in s_sel:
            m = jnp.maximum(m, jnp.max(s, axis=0, keepdims=True))
        p_own = jnp.exp(s_own - m)
        l = jnp.sum(p_own, axis=0, keepdims=True)
        o = jnp.sum(p_own * vn, axis=0, keepdims=True)
        for p in range(n_sel):
            pp = jnp.exp(s_sel[p] - m)
            l = l + jnp.sum(pp, axis=0, keepdims=True)
            o = o + jnp.sum(pp * v_pages[t * n_sel + p][0], axis=0, keepdims=True)
        o_ref[0, t:t + 1, :] = o / l


def _sample_attn(q_s, k_s, v_s, cache_k_pages, cache_v_pages, pt_flat, idx_flat, page0, n_pages):
    db, t_new, _ = q_s.shape
    n_sel = TOP_K * PAGES_PER_BLOCK

    def page_map(bi, h, pt, idx, *, t, k, i):
        blk = idx[((bi * t_new + t) * TOP_K + k) * N_HEADS + h]
        return (page0 + pt[bi * n_pages + blk * PAGES_PER_BLOCK + i], 0, h)

    page_specs = [pl.BlockSpec((1, PAGE_SIZE, HEAD_DIM), functools.partial(page_map, t=t, k=k, i=i))
                  for t in range(t_new) for k in range(TOP_K) for i in range(PAGES_PER_BLOCK)]
    new_spec = pl.BlockSpec((1, t_new, HEAD_DIM), lambda bi, h, pt, idx: (bi, 0, h))
    return pl.pallas_call(
        functools.partial(_sample_attn_kernel, t_new=t_new, n_sel=n_sel),
        grid_spec=pltpu.PrefetchScalarGridSpec(
            num_scalar_prefetch=2,
            grid=(db, N_HEADS),
            in_specs=[new_spec, new_spec, new_spec] + page_specs + page_specs,
            out_specs=new_spec,
        ),
        out_shape=jax.ShapeDtypeStruct((db, t_new, D_ATTN), F32),
        compiler_params=pltpu.CompilerParams(dimension_semantics=("parallel", "parallel")),
        name="sample_attn",
    )(pt_flat, idx_flat, q_s, k_s, v_s, *([cache_k_pages] * len(page_specs)),
      *([cache_v_pages] * len(page_specs)))


def _merge_kernel(x_ref, pool_ref, attn_ref, ga_ref, gb_ref, wup_ref, wua_ref, wo_ref, g_ref, b_ref, o_ref,
                  *, alpha):
    pu = jnp.dot(pool_ref[...], wup_ref[...], preferred_element_type=F32)
    au = jnp.dot(attn_ref[...].astype(BF16), wua_ref[...], preferred_element_type=F32)
    mix = jax.nn.sigmoid(ga_ref[...]) * pu + jax.nn.sigmoid(gb_ref[...]) * au
    z = alpha * x_ref[...] + jnp.dot(mix.astype(BF16), wo_ref[...], preferred_element_type=F32)
    o_ref[...] = _layer_norm(z, g_ref[...], b_ref[...])


def _merge(x, pool, attn, ga, gb, wup_b, wua_b, wo_b, ln_g, ln_b, alpha):
    m, d_model = x.shape
    tm = ROW_TILE
    row = lambda i: (i, 0)
    return pl.pallas_call(
        functools.partial(_merge_kernel, alpha=alpha),
        grid=(m // tm,),
        in_specs=[pl.BlockSpec((tm, d_model), row), pl.BlockSpec((tm, pool.shape[1]), row),
                  pl.BlockSpec((tm, attn.shape[1]), row), pl.BlockSpec((tm, d_model), row),
                  pl.BlockSpec((tm, d_model), row), _resident(wup_b.shape), _resident(wua_b.shape),
                  _resident(wo_b.shape), _resident(ln_g.shape), _resident(ln_b.shape)],
        out_specs=pl.BlockSpec((tm, d_model), row),
        out_shape=jax.ShapeDtypeStruct((m, d_model), F32),
        compiler_params=pltpu.CompilerParams(dimension_semantics=("parallel",), vmem_limit_bytes=VMEM_LIMIT),
        name="merge_ln",
    )(x, pool, attn, ga, gb, wup_b, wua_b, wo_b, ln_g, ln_b)


def _ffn_kernel(x_ref, wg_ref, wu_ref, wd_ref, g_ref, b_ref, o_ref, *, alpha):
    x = x_ref[...]
    xb = x.astype(BF16)
    gate = jnp.dot(xb, wg_ref[...], preferred_element_type=F32)
    up = jnp.dot(xb, wu_ref[...], preferred_element_type=F32)
    h = (jax.nn.silu(gate) * up).astype(BF16)
    z = alpha * x + jnp.dot(h, wd_ref[...], preferred_element_type=F32)
    o_ref[...] = _layer_norm(z, g_ref[...], b_ref[...])


def _ffn(x, wg_b, wu_b, wd_b, ln_g, ln_b, alpha):
    m, d_model = x.shape
    tm = ROW_TILE
    row = lambda i: (i, 0)
    return pl.pallas_call(
        functools.partial(_ffn_kernel, alpha=alpha),
        grid=(m // tm,),
        in_specs=[pl.BlockSpec((tm, d_model), row), _resident(wg_b.shape), _resident(wu_b.shape),
                  _resident(wd_b.shape), _resident(ln_g.shape), _resident(ln_b.shape)],
        out_specs=pl.BlockSpec((tm, d_model), row),
        out_shape=jax.ShapeDtypeStruct((m, d_model), F32),
        compiler_params=pltpu.CompilerParams(dimension_semantics=("parallel",), vmem_limit_bytes=VMEM_LIMIT),
        name="ffn_ln",
    )(x, wg_b, wu_b, wd_b, ln_g, ln_b)


def kernel(x_prompt, x_sample, cache_k, cache_v, state_pool, page_table, w_in, w_pool, pool_scale, w_up_pool,
           w_up_attn, w_out, ln1_g, ln1_b, w_gate, w_up, w_down, ln2_g, ln2_b):
    depth = w_in.shape[0]
    batch, seq, d_model = x_prompt.shape
    db, t_new, _ = x_sample.shape
    d_pool = w_pool.shape[1] * w_pool.shape[2]
    n_phys = cache_k.shape[1]
    n_pages = page_table.shape[1]
    past_len = n_pages * PAGE_SIZE
    assert seq % BLOCK_SIZE == 0 and past_len % BLOCK_SIZE == 0 and n_pages // PAGES_PER_BLOCK >= TOP_K
    assert seq >= POOL_STATE_LEN and past_len >= POOL_STATE_LEN and t_new <= POOL_STATE_LEN
    alpha = (2.0 * depth) ** 0.25

    cos_p, sin_p = _rope_tables(jnp.arange(seq))
    cos_s, sin_s = _rope_tables(past_len + jnp.arange(db * t_new) % t_new)
    cache_k_pages = cache_k.reshape(depth * n_phys, PAGE_SIZE, D_ATTN)
    cache_v_pages = cache_v.reshape(depth * n_phys, PAGE_SIZE, D_ATTN)
    pt_flat = page_table.reshape(-1)

    yp = x_prompt.reshape(batch * seq, d_model)
    ys = x_sample.reshape(db * t_new, d_model)
    outs = [[] for _ in range(6)]
    for l in range(depth):
        w_in_b = w_in[l].astype(BF16)
        w_pool_b = w_pool[l].astype(BF16)
        ps = pool_scale[l].reshape(1, d_pool)
        wup_b, wua_b, wo_b = w_up_pool[l].astype(BF16), w_up_attn[l].astype(BF16), w_out[l].astype(BF16)
        wg_b, wu_b, wd_b = w_gate[l].astype(BF16), w_up[l].astype(BF16), w_down[l].astype(BF16)
        g1, b1 = ln1_g[l].reshape(1, d_model), ln1_b[l].reshape(1, d_model)
        g2, b2 = ln2_g[l].reshape(1, d_model), ln2_b[l].reshape(1, d_model)

        u, q_b, k_f, k_b, v_f, v_b, ga, gb, kmean = _proj(
            yp, w_in_b, cos_p, sin_p, d_pool=d_pool, q_dtype=BF16, q_scale=SCALE, block_means=True)
        pool = _pool_prompt(u.reshape(batch, seq, d_pool), w_pool_b, ps)
        attn = _moba_prompt(q_b.reshape(batch, seq, D_ATTN), k_b.reshape(batch, seq, D_ATTN),
                            v_b.reshape(batch, seq, D_ATTN), kmean.reshape(batch, seq // BLOCK_SIZE, D_ATTN))
        x1 = _merge(yp, pool.reshape(batch * seq, d_pool), attn.reshape(batch * seq, D_ATTN), ga, gb,
                    wup_b, wua_b, wo_b, g1, b1, alpha)
        yp = _ffn(x1, wg_b, wu_b, wd_b, g2, b2, alpha)
        outs[0].append(k_f.reshape(batch, seq, N_HEADS, HEAD_DIM))
        outs[1].append(v_f.reshape(batch, seq, N_HEADS, HEAD_DIM))
        outs[2].append(u.reshape(batch, seq, d_pool)[:, seq - POOL_STATE_LEN:])

        u_s, q_s, k_s, _, v_s, _, ga_s, gb_s = _proj(
            ys, w_in_b, cos_s, sin_s, d_pool=d_pool, q_dtype=F32, q_scale=1.0, block_means=False)
        u_s3 = u_s.reshape(db, t_new, d_pool)
        pool_s = _pool_sample(state_pool[l].transpose(1, 0, 2), u_s3.transpose(1, 0, 2), w_pool_b, ps)
        pool_s = pool_s.transpose(1, 0, 2)
        q_s3, k_s3, v_s3 = (a.reshape(db, t_new, D_ATTN) for a in (q_s, k_s, v_s))
        kmean_s = _cache_block_mean(cache_k_pages, pt_flat, l * n_phys, db, n_pages)
        idx = _sample_topk(q_s3, kmean_s)
        attn_s = _sample_attn(q_s3, k_s3, v_s3, cache_k_pages, cache_v_pages, pt_flat, idx.reshape(-1),
                              l * n_phys, n_pages)
        x1_s = _merge(ys, pool_s.reshape(db * t_new, d_pool), attn_s.reshape(db * t_new, D_ATTN), ga_s, gb_s,
                      wup_b, wua_b, wo_b, g1, b1, alpha)
        ys = _ffn(x1_s, wg_b, wu_b, wd_b, g2, b2, alpha)
        outs[3].append(k_s3.reshape(db, t_new, N_HEADS, HEAD_DIM))
        outs[4].append(v_s3.reshape(db, t_new, N_HEADS, HEAD_DIM))
        outs[5].append(jnp.concatenate([state_pool[l], u_s3], axis=1)[:, t_new:])

    return (yp.reshape(batch, seq, d_model), ys.reshape(db, t_new, d_model),
            jnp.stack(outs[0]), jnp.stack(outs[1]), jnp.stack(outs[2]),
            jnp.stack(outs[3]), jnp.stack(outs[4]), jnp.stack(outs[5]))
```

```python
import functools

import jax
import jax.numpy as jnp
from jax import lax
from jax.experimental import pallas as pl
from jax.experimental.pallas import tpu as pltpu

F32 = jnp.float32
BF16 = jnp.bfloat16

N_HEADS = 8
HEAD_DIM = 128
D_ATTN = N_HEADS * HEAD_DIM
BLOCK_SIZE = 256
TOP_K = 3
PAGE_SIZE = 128
PAGES_PER_BLOCK = BLOCK_SIZE // PAGE_SIZE
ROPE_THETA = 10000.0
POOL_WINDOWS = (2, 4, 8, 16)
POOL_STATE_LEN = max(POOL_WINDOWS) - 1
POOL_HALO = 16
LN_EPS = 1e-5
NEG_INF = -1e30
SCALE = HEAD_DIM ** -0.5
LOG2_E = 1.4426950408889634

ROW_TILE = 256
POOL_TILE = 512
PAGES_PER_STEP = 16
MOBA_PAD = 4
VMEM_LIMIT = 56 * 1024 * 1024

NT_DIMS = (((1,), (1,)), ((), ()))


def _resident(shape):
    return pl.BlockSpec(shape, lambda *_: (0,) * len(shape), pipeline_mode=pl.Buffered(1))


def _layer_norm(z, g, b):
    mu = jnp.mean(z, axis=-1, keepdims=True)
    zc = z - mu
    var = jnp.mean(zc * zc, axis=-1, keepdims=True)
    return zc * lax.rsqrt(var + LN_EPS) * g + b


def _proj_kernel(x_ref, w_ref, cos_ref, sin_ref, u_ref, q_ref, kf_ref, kb_ref, vf_ref, vb_ref,
                 ga_ref, gb_ref, *km_ref, d_pool, q_scale):
    xb = x_ref[...].astype(BF16)
    cos = cos_ref[...]
    sin = sin_ref[...]

    def proj(c0, width):
        return jnp.dot(xb, w_ref[:, c0:c0 + width], preferred_element_type=F32)

    def rope(h):
        return h * cos + pltpu.roll(h, HEAD_DIM // 2, 1) * sin

    u_ref[...] = proj(0, d_pool)
    half = D_ATTN // 2
    for c in range(2):
        acc = proj(d_pool + c * half, half)
        for hh in range(half // HEAD_DIM):
            sl = slice(c * half + hh * HEAD_DIM, c * half + (hh + 1) * HEAD_DIM)
            r = rope(acc[:, hh * HEAD_DIM:(hh + 1) * HEAD_DIM])
            q_ref[:, sl] = (r * q_scale).astype(q_ref.dtype)
    for c in range(2):
        acc = proj(d_pool + D_ATTN + c * half, half)
        for hh in range(half // HEAD_DIM):
            sl = slice(c * half + hh * HEAD_DIM, c * half + (hh + 1) * HEAD_DIM)
            r = rope(acc[:, hh * HEAD_DIM:(hh + 1) * HEAD_DIM])
            kf_ref[:, sl] = r
            kb_ref[:, sl] = r.astype(BF16)
            if km_ref:
                km_ref[0][0, :, sl] = jnp.sum(r, axis=0, keepdims=True) * (1.0 / r.shape[0])
    for c in range(2):
        acc = proj(d_pool + 2 * D_ATTN + c * half, half)
        vf_ref[:, c * half:(c + 1) * half] = acc
        vb_ref[:, c * half:(c + 1) * half] = acc.astype(BF16)
    d_model = ga_ref.shape[1]
    for c in range(2):
        hw = d_model // 2
        ga_ref[:, c * hw:(c + 1) * hw] = proj(d_pool + 3 * D_ATTN + c * hw, hw)
        gb_ref[:, c * hw:(c + 1) * hw] = proj(d_pool + 3 * D_ATTN + d_model + c * hw, hw)


def _proj(x, w_in_b, cos, sin, *, d_pool, q_dtype, q_scale, block_means):
    m, d_model = x.shape
    n_in = w_in_b.shape[1]
    tm = ROW_TILE
    n_tab = cos.shape[0] // tm
    row = lambda i: (i, 0)
    out_shape = [
        jax.ShapeDtypeStruct((m, d_pool), F32),
        jax.ShapeDtypeStruct((m, D_ATTN), q_dtype),
        jax.ShapeDtypeStruct((m, D_ATTN), F32),
        jax.ShapeDtypeStruct((m, D_ATTN), BF16),
        jax.ShapeDtypeStruct((m, D_ATTN), F32),
        jax.ShapeDtypeStruct((m, D_ATTN), BF16),
        jax.ShapeDtypeStruct((m, d_model), F32),
        jax.ShapeDtypeStruct((m, d_model), F32),
    ]
    out_specs = [
        pl.BlockSpec((tm, d_pool), row),
        pl.BlockSpec((tm, D_ATTN), row),
        pl.BlockSpec((tm, D_ATTN), row),
        pl.BlockSpec((tm, D_ATTN), row),
        pl.BlockSpec((tm, D_ATTN), row),
        pl.BlockSpec((tm, D_ATTN), row),
        pl.BlockSpec((tm, d_model), row),
        pl.BlockSpec((tm, d_model), row),
    ]
    if block_means:
        out_shape.append(jax.ShapeDtypeStruct((m // tm, 1, D_ATTN), F32))
        out_specs.append(pl.BlockSpec((1, 1, D_ATTN), lambda i: (i, 0, 0)))
    return pl.pallas_call(
        functools.partial(_proj_kernel, d_pool=d_pool, q_scale=q_scale),
        grid=(m // tm,),
        in_specs=[
            pl.BlockSpec((tm, d_model), row),
            _resident((d_model, n_in)),
            pl.BlockSpec((tm, HEAD_DIM), lambda i: (i % n_tab, 0)),
            pl.BlockSpec((tm, HEAD_DIM), lambda i: (i % n_tab, 0)),
        ],
        out_specs=out_specs,
        out_shape=out_shape,
        compiler_params=pltpu.CompilerParams(dimension_semantics=("parallel",), vmem_limit_bytes=VMEM_LIMIT),
        name="proj_rope",
    )(x, w_in_b, cos, sin)


def _rope_tables(pos):
    half = HEAD_DIM // 2
    inv = ROPE_THETA ** (-jnp.arange(half, dtype=F32) / half)
    ang = pos.astype(F32)[:, None] * inv
    cos, sin = jnp.cos(ang), jnp.sin(ang)
    return jnp.concatenate([cos, cos], -1), jnp.concatenate([-sin, sin], -1)


def _pool_prompt_kernel(u_ref, halo_ref, wp_ref, ps_ref, o_ref, ext_scr):
    i = pl.program_id(1)
    tt = u_ref.shape[1]
    gd = wp_ref.shape[1]
    ext_scr[0:POOL_HALO, :] = jnp.where(i == 0, 0.0, halo_ref[0])
    ext_scr[POOL_HALO:, :] = u_ref[0]
    pos = i * tt + lax.broadcasted_iota(jnp.int32, (tt, 1), 0)
    for g, w in enumerate(POOL_WINDOWS):
        cols = slice(g * gd, (g + 1) * gd)
        win = ext_scr[POOL_HALO:POOL_HALO + tt, cols]
        for back in range(1, w):
            win = win + ext_scr[POOL_HALO - back:POOL_HALO - back + tt, cols]
        cnt = jnp.minimum(pos + 1, w).astype(F32)
        d = win / cnt - ext_scr[POOL_HALO:POOL_HALO + tt, cols]
        y = jnp.dot(d.astype(BF16), wp_ref[g], preferred_element_type=F32) * ps_ref[:, cols]
        o_ref[0, :, cols] = y.astype(o_ref.dtype)


def _pool_prompt(u, w_pool_b, pool_scale):
    b, t, d_pool = u.shape
    tt = POOL_TILE
    halo_blocks = tt // POOL_HALO
    return pl.pallas_call(
        _pool_prompt_kernel,
        grid=(b, t // tt),
        in_specs=[
            pl.BlockSpec((1, tt, d_pool), lambda bi, i: (bi, i, 0)),
            pl.BlockSpec((1, POOL_HALO, d_pool), lambda bi, i: (bi, jnp.maximum(i * halo_blocks - 1, 0), 0)),
            _resident(w_pool_b.shape),
            _resident(pool_scale.shape),
        ],
        out_specs=pl.BlockSpec((1, tt, d_pool), lambda bi, i: (bi, i, 0)),
        out_shape=jax.ShapeDtypeStruct((b, t, d_pool), BF16),
        scratch_shapes=[pltpu.VMEM((POOL_HALO + tt, d_pool), F32)],
        compiler_params=pltpu.CompilerParams(dimension_semantics=("parallel", "parallel")),
        name="pool_prompt",
    )(u, u, w_pool_b, pool_scale)


def _pool_sample_kernel(st_ref, u_ref, wp_ref, ps_ref, o_ref):
    t_new = u_ref.shape[0]
    gd = wp_ref.shape[1]
    p = POOL_STATE_LEN

    def ext_row(j, cols):
        return st_ref[j, :, cols] if j < p else u_ref[j - p, :, cols]

    for g, w in enumerate(POOL_WINDOWS):
        cols = slice(g * gd, (g + 1) * gd)
        for t in range(t_new):
            win = ext_row(p + t, cols)
            for back in range(1, w):
                win = win + ext_row(p + t - back, cols)
            d = win / float(w) - ext_row(p + t, cols)
            y = jnp.dot(d.astype(BF16), wp_ref[g], preferred_element_type=F32) * ps_ref[:, cols]
            o_ref[t, :, cols] = y.astype(o_ref.dtype)


def _pool_sample(state_t, u_t, w_pool_b, pool_scale):
    t_new, db, d_pool = u_t.shape
    return pl.pallas_call(
        _pool_sample_kernel,
        grid=(1,),
        in_specs=[_resident(state_t.shape), _resident(u_t.shape), _resident(w_pool_b.shape),
                  _resident(pool_scale.shape)],
        out_specs=pl.BlockSpec((t_new, db, d_pool), lambda i: (0, 0, 0)),
        out_shape=jax.ShapeDtypeStruct((t_new, db, d_pool), BF16),
        name="pool_sample",
    )(state_t, u_t, w_pool_b, pool_scale)


def _select_blocks(g, n_valid, k_keep):
    nb = g.shape[0]
    n_iota = lax.broadcasted_iota(jnp.int32, g.shape, 0)
    valid = n_iota < n_valid
    g = jnp.where(valid, g, NEG_INF)
    rank = jnp.zeros(g.shape, jnp.int32)
    for m in range(nb):
        row = g[m:m + 1, :]
        ahead = (row > g) | ((row == g) & (n_iota > m))
        rank = rank + ahead.astype(jnp.int32)
    return rank, valid & (rank < k_keep)


def _fold_rows(x, op):
    parts = [x[i:i + 8] for i in range(0, x.shape[0], 8)]
    while len(parts) > 1:
        parts = [op(parts[i], parts[i + 1]) for i in range(0, len(parts), 2)]
    return parts[0]


def _moba_prompt_kernel(q_ref, k_ref, vt_ref, km_ref, o_ref, sel_scr, s_scr, p_scr):
    qi = pl.program_id(2)
    nb = km_ref.shape[1]
    bs = BLOCK_SIZE
    q = q_ref[0]
    km = km_ref[0]
    km_hi = km.astype(BF16)
    km_lo = (km - km_hi.astype(F32)).astype(BF16)
    gate = (lax.dot_general(km_hi, q, NT_DIMS, preferred_element_type=F32)
            + lax.dot_general(km_lo, q, NT_DIMS, preferred_element_type=F32))
    _, sel = _select_blocks(gate, qi, jnp.minimum(TOP_K, qi))
    sel_scr[...] = sel.astype(F32)

    def rows(j):
        return pl.ds(pl.multiple_of(j * bs, bs), bs)

    def own():
        key_i = lax.broadcasted_iota(jnp.int32, (bs, bs), 0)
        qry_i = lax.broadcasted_iota(jnp.int32, (bs, bs), 1)
        return jnp.where(key_i <= qry_i, s_scr[rows(qi), :], NEG_INF)

    def attend(n_pad):
        nk = n_pad * bs
        s_scr[0:nk, :] = lax.dot_general(k_ref[0, 0:nk, :], q, NT_DIMS, preferred_element_type=F32)

        def mask_max(j, m8):
            s = jnp.where(sel_scr[pl.ds(j, 1), :] > 0.5, s_scr[rows(j), :], NEG_INF)
            s_scr[rows(j), :] = s
            return jnp.maximum(m8, _fold_rows(s, jnp.maximum))

        m8 = lax.fori_loop(0, qi, mask_max, _fold_rows(own(), jnp.maximum))
        m = jnp.max(m8, axis=0, keepdims=True)

        def weigh(j, l8):
            p = jnp.exp2(s_scr[rows(j), :] - m)
            p_scr[rows(j), :] = p.astype(BF16)
            return l8 + _fold_rows(p, jnp.add)

        p_own = jnp.exp2(own() - m)
        p_scr[rows(qi), :] = p_own.astype(BF16)
        l8 = lax.fori_loop(0, qi, weigh, _fold_rows(p_own, jnp.add))

        def clear(j, carry):
            p_scr[rows(j), :] = jnp.zeros((bs, bs), BF16)
            return carry

        lax.fori_loop(qi + 1, n_pad, clear, 0)
        l = jnp.sum(l8, axis=0, keepdims=True)
        acc = jnp.dot(vt_ref[0, 0, :, 0:nk], p_scr[0:nk, :], preferred_element_type=F32)
        o_ref[0] = (acc / l).T.astype(o_ref.dtype)

    for n_pad in range(MOBA_PAD, nb + 1, MOBA_PAD):
        pl.when((qi + 1 > n_pad - MOBA_PAD) & (qi + 1 <= n_pad))(functools.partial(attend, n_pad))


def _moba_prompt(q_b, k_b, v_b, kmean):
    b, t, _ = q_b.shape
    nb = t // BLOCK_SIZE
    assert nb % MOBA_PAD == 0
    v_t = v_b.reshape(b, t, N_HEADS, HEAD_DIM).transpose(0, 2, 3, 1)
    return pl.pallas_call(
        _moba_prompt_kernel,
        grid=(b, N_HEADS, nb),
        in_specs=[
            pl.BlockSpec((1, BLOCK_SIZE, HEAD_DIM), lambda bi, h, i: (bi, i, h)),
            pl.BlockSpec((1, t, HEAD_DIM), lambda bi, h, i: (bi, 0, h)),
            pl.BlockSpec((1, 1, HEAD_DIM, t), lambda bi, h, i: (bi, h, 0, 0)),
            pl.BlockSpec((1, nb, HEAD_DIM), lambda bi, h, i: (bi, 0, h)),
        ],
        out_specs=pl.BlockSpec((1, BLOCK_SIZE, HEAD_DIM), lambda bi, h, i: (bi, i, h)),
        out_shape=jax.ShapeDtypeStruct((b, t, D_ATTN), BF16),
        scratch_shapes=[pltpu.VMEM((nb, BLOCK_SIZE), F32), pltpu.VMEM((t, BLOCK_SIZE), F32),
                        pltpu.VMEM((t, BLOCK_SIZE), BF16)],
        compiler_params=pltpu.CompilerParams(dimension_semantics=("parallel", "parallel", "arbitrary"),
                                             vmem_limit_bytes=VMEM_LIMIT),
        name="moba_prompt",
    )(q_b, k_b, v_t, kmean)


def _cache_block_mean_kernel(pt_ref, *refs):
    pages, o_ref = refs[:-1], refs[-1]
    for n in range(len(pages) // PAGES_PER_BLOCK):
        tot = jnp.sum(pages[PAGES_PER_BLOCK * n][...], axis=0)
        for i in range(1, PAGES_PER_BLOCK):
            tot = tot + jnp.sum(pages[PAGES_PER_BLOCK * n + i][...], axis=0)
        tot = tot * (1.0 / BLOCK_SIZE)
        for h in range(N_HEADS):
            o_ref[0, n:n + 1, h * HEAD_DIM:(h + 1) * HEAD_DIM] = tot[h:h + 1, :]


def _cache_block_mean(cache, pt_flat, layer, db, n_pages):
    steps = n_pages // PAGES_PER_STEP
    blocks_per_step = PAGES_PER_STEP // PAGES_PER_BLOCK

    def page_map(bi, c, pt, *, i):
        return (layer, pt[bi * n_pages + c * PAGES_PER_STEP + i], 0, 0, 0)

    return pl.pallas_call(
        _cache_block_mean_kernel,
        grid_spec=pltpu.PrefetchScalarGridSpec(
            num_scalar_prefetch=1,
            grid=(db, steps),
            in_specs=[pl.BlockSpec((None, None, PAGE_SIZE, N_HEADS, HEAD_DIM), functools.partial(page_map, i=i))
                      for i in range(PAGES_PER_STEP)],
            out_specs=pl.BlockSpec((1, blocks_per_step, D_ATTN), lambda bi, c, pt: (bi, c, 0)),
        ),
        out_shape=jax.ShapeDtypeStruct((db, n_pages // PAGES_PER_BLOCK, D_ATTN), F32),
        compiler_params=pltpu.CompilerParams(dimension_semantics=("parallel", "arbitrary"),
                                             vmem_limit_bytes=VMEM_LIMIT),
        name="cache_block_mean",
    )(pt_flat, *([cache] * PAGES_PER_STEP))


def _sample_topk_kernel(q_ref, km_ref, idx_ref):
    q = q_ref[0]
    km = km_ref[0]
    nb = km.shape[0]
    lane = lax.broadcasted_iota(jnp.int32, (nb, 128), 1)
    blk = lax.broadcasted_iota(jnp.int32, (nb, 128), 0).astype(F32)
    idx_ref[...] = jnp.zeros(idx_ref.shape, jnp.int32)
    for t in range(q.shape[0]):
        prod = km * q[t:t + 1, :]
        gate = jnp.zeros((nb, 128), F32)
        for h in range(N_HEADS):
            s = jnp.sum(prod[:, h * HEAD_DIM:(h + 1) * HEAD_DIM], axis=1, keepdims=True)
            gate = jnp.where(lane == h, s, gate)
        rank, _ = _select_blocks(gate, nb, TOP_K)
        for k in range(TOP_K):
            idx_ref[0, TOP_K * t + k:TOP_K * t + k + 1, :] = jnp.sum(
                jnp.where(rank == k, blk, 0.0), axis=0, keepdims=True).astype(jnp.int32)


def _sample_topk(q_s, kmean):
    db, t_new, _ = q_s.shape
    nb = kmean.shape[1]
    rows = -(-t_new * TOP_K // 8) * 8
    out = pl.pallas_call(
        _sample_topk_kernel,
        grid=(db,),
        in_specs=[pl.BlockSpec((1, t_new, D_ATTN), lambda bi: (bi, 0, 0)),
                  pl.BlockSpec((1, nb, D_ATTN), lambda bi: (bi, 0, 0))],
        out_specs=pl.BlockSpec((1, rows, 128), lambda bi: (bi, 0, 0)),
        out_shape=jax.ShapeDtypeStruct((db, rows, 128), jnp.int32),
        compiler_params=pltpu.CompilerParams(dimension_semantics=("parallel",)),
        name="sample_topk",
    )(q_s, kmean)
    return out[:, :t_new * TOP_K, :N_HEADS].reshape(db, t_new, TOP_K, N_HEADS)


def _sample_attn_kernel(pt_ref, idx_ref, q_ref, kn_ref, vn_ref, ck_hbm, cv_hbm, o_ref, kbuf, vbuf, sem,
                        *, t_new, n_pages, layer):
    n_sel = TOP_K * PAGES_PER_BLOCK
    step = pl.program_id(0) * N_HEADS + pl.program_id(1)
    n_steps = pl.num_programs(0) * N_HEADS
    slot = step % 2

    def page_copies(page, h, slot_, j):
        return (pltpu.make_async_copy(ck_hbm.at[layer, page, :, h, :], kbuf.at[slot_, j], sem.at[0, slot_]),
                pltpu.make_async_copy(cv_hbm.at[layer, page, :, h, :], vbuf.at[slot_, j], sem.at[1, slot_]))

    def start_gather(step_, slot_):
        b_ = step_ // N_HEADS
        h_ = step_ % N_HEADS
        for t in range(t_new):
            for k in range(TOP_K):
                blk = idx_ref[((b_ * t_new + t) * TOP_K + k) * N_HEADS + h_]
                for i in range(PAGES_PER_BLOCK):
                    page = pt_ref[b_ * n_pages + blk * PAGES_PER_BLOCK + i]
                    for c in page_copies(page, h_, slot_, (t * TOP_K + k) * PAGES_PER_BLOCK + i):
                        c.start()

    @pl.when(step == 0)
    def _():
        start_gather(step, slot)

    @pl.when(step + 1 < n_steps)
    def _():
        start_gather(step + 1, 1 - slot)

    for j in range(t_new * n_sel):
        for c in page_copies(0, 0, slot, j):
            c.wait()

    q = q_ref[0]
    kn = kn_ref[0]
    vn = vn_ref[0]
    own_i = lax.broadcasted_iota(jnp.int32, (t_new, 1), 0)
    for t in range(t_new):
        qt = q[t:t + 1, :]
        s_sel = [jnp.sum(kbuf[slot, t * n_sel + p] * qt, axis=1, keepdims=True) * SCALE for p in range(n_sel)]
        s_own = jnp.sum(kn * qt, axis=1, keepdims=True) * SCALE
        s_own = jnp.where(own_i <= t, s_own, NEG_INF)
        m = jnp.max(s_own, axis=0, keepdims=True)
        for s in s_sel:
            m = jnp.maximum(m, jnp.max(s, axis=0, keepdims=True))
        p_own = jnp.exp(s_own - m)
        l = jnp.sum(p_own, axis=0, keepdims=True)
        o = jnp.sum(p_own * vn, axis=0, keepdims=True)
        for p in range(n_sel):
            pp = jnp.exp(s_sel[p] - m)
            l = l + jnp.sum(pp, axis=0, keepdims=True)
            o = o + jnp.sum(pp * vbuf[slot, t * n_sel + p], axis=0, keepdims=True)
        o_ref[0, t:t + 1, :] = o / l


def _sample_attn(q_s, k_s, v_s, cache_k, cache_v, pt_flat, idx_flat, layer, n_pages):
    db, t_new, _ = q_s.shape
    n_slices = t_new * TOP_K * PAGES_PER_BLOCK
    new_spec = pl.BlockSpec((1, t_new, HEAD_DIM), lambda bi, h, pt, idx: (bi, 0, h))
    return pl.pallas_call(
        functools.partial(_sample_attn_kernel, t_new=t_new, n_pages=n_pages, layer=layer),
        grid_spec=pltpu.PrefetchScalarGridSpec(
            num_scalar_prefetch=2,
            grid=(db, N_HEADS),
            in_specs=[new_spec, new_spec, new_spec, pl.BlockSpec(memory_space=pl.ANY),
                      pl.BlockSpec(memory_space=pl.ANY)],
            out_specs=new_spec,
            scratch_shapes=[pltpu.VMEM((2, n_slices, PAGE_SIZE, HEAD_DIM), F32),
                            pltpu.VMEM((2, n_slices, PAGE_SIZE, HEAD_DIM), F32),
                            pltpu.SemaphoreType.DMA((2, 2))],
        ),
        out_shape=jax.ShapeDtypeStruct((db, t_new, D_ATTN), F32),
        compiler_params=pltpu.CompilerParams(dimension_semantics=("arbitrary", "arbitrary")),
        name="sample_attn",
    )(pt_flat, idx_flat, q_s, k_s, v_s, cache_k, cache_v)


def _merge_kernel(x_ref, pool_ref, attn_ref, ga_ref, gb_ref, wup_ref, wua_ref, wo_ref, g_ref, b_ref, o_ref,
                  *, alpha):
    pu = jnp.dot(pool_ref[...], wup_ref[...], preferred_element_type=F32)
    au = jnp.dot(attn_ref[...].astype(BF16), wua_ref[...], preferred_element_type=F32)
    mix = jax.nn.sigmoid(ga_ref[...]) * pu + jax.nn.sigmoid(gb_ref[...]) * au
    z = alpha * x_ref[...] + jnp.dot(mix.astype(BF16), wo_ref[...], preferred_element_type=F32)
    o_ref[...] = _layer_norm(z, g_ref[...], b_ref[...])


def _merge(x, pool, attn, ga, gb, wup_b, wua_b, wo_b, ln_g, ln_b, alpha):
    m, d_model = x.shape
    tm = ROW_TILE
    row = lambda i: (i, 0)
    return pl.pallas_call(
        functools.partial(_merge_kernel, alpha=alpha),
        grid=(m // tm,),
        in_specs=[pl.BlockSpec((tm, d_model), row), pl.BlockSpec((tm, pool.shape[1]), row),
                  pl.BlockSpec((tm, attn.shape[1]), row), pl.BlockSpec((tm, d_model), row),
                  pl.BlockSpec((tm, d_model), row), _resident(wup_b.shape), _resident(wua_b.shape),
                  _resident(wo_b.shape), _resident(ln_g.shape), _resident(ln_b.shape)],
        out_specs=pl.BlockSpec((tm, d_model), row),
        out_shape=jax.ShapeDtypeStruct((m, d_model), F32),
        compiler_params=pltpu.CompilerParams(dimension_semantics=("parallel",), vmem_limit_bytes=VMEM_LIMIT),
        name="merge_ln",
    )(x, pool, attn, ga, gb, wup_b, wua_b, wo_b, ln_g, ln_b)


def _ffn_kernel(x_ref, wg_ref, wu_ref, wd_ref, g_ref, b_ref, o_ref, *, alpha):
    x = x_ref[...]
    xb = x.astype(BF16)
    gate = jnp.dot(xb, wg_ref[...], preferred_element_type=F32)
    up = jnp.dot(xb, wu_ref[...], preferred_element_type=F32)
    h = (jax.nn.silu(gate) * up).astype(BF16)
    z = alpha * x + jnp.dot(h, wd_ref[...], preferred_element_type=F32)
    o_ref[...] = _layer_norm(z, g_ref[...], b_ref[...])


def _ffn(x, wg_b, wu_b, wd_b, ln_g, ln_b, alpha):
    m, d_model = x.shape
    tm = ROW_TILE
    row = lambda i: (i, 0)
    return pl.pallas_call(
        functools.partial(_ffn_kernel, alpha=alpha),
        grid=(m // tm,),
        in_specs=[pl.BlockSpec((tm, d_model), row), _resident(wg_b.shape), _resident(wu_b.shape),
                  _resident(wd_b.shape), _resident(ln_g.shape), _resident(ln_b.shape)],
        out_specs=pl.BlockSpec((tm, d_model), row),
        out_shape=jax.ShapeDtypeStruct((m, d_model), F32),
        compiler_params=pltpu.CompilerParams(dimension_semantics=("parallel",), vmem_limit_bytes=VMEM_LIMIT),
        name="ffn_ln",
    )(x, wg_b, wu_b, wd_b, ln_g, ln_b)


def kernel(x_prompt, x_sample, cache_k, cache_v, state_pool, page_table, w_in, w_pool, pool_scale, w_up_pool,
           w_up_attn, w_out, ln1_g, ln1_b, w_gate, w_up, w_down, ln2_g, ln2_b):
    depth = w_in.shape[0]
    batch, seq, d_model = x_prompt.shape
    db, t_new, _ = x_sample.shape
    d_pool = w_pool.shape[1] * w_pool.shape[2]
    n_phys = cache_k.shape[1]
    n_pages = page_table.shape[1]
    past_len = n_pages * PAGE_SIZE
    assert seq % BLOCK_SIZE == 0 and past_len % BLOCK_SIZE == 0 and n_pages // PAGES_PER_BLOCK >= TOP_K
    assert seq >= POOL_STATE_LEN and past_len >= POOL_STATE_LEN and t_new <= POOL_STATE_LEN
    alpha = (2.0 * depth) ** 0.25

    cos_p, sin_p = _rope_tables(jnp.arange(seq))
    cos_s, sin_s = _rope_tables(past_len + jnp.arange(db * t_new) % t_new)
    pt_flat = page_table.reshape(-1)

    yp = x_prompt.reshape(batch * seq, d_model)
    ys = x_sample.reshape(db * t_new, d_model)
    outs = [[] for _ in range(6)]
    for l in range(depth):
        w_in_b = w_in[l].astype(BF16)
        w_pool_b = w_pool[l].astype(BF16)
        ps = pool_scale[l].reshape(1, d_pool)
        wup_b, wua_b, wo_b = w_up_pool[l].astype(BF16), w_up_attn[l].astype(BF16), w_out[l].astype(BF16)
        wg_b, wu_b, wd_b = w_gate[l].astype(BF16), w_up[l].astype(BF16), w_down[l].astype(BF16)
        g1, b1 = ln1_g[l].reshape(1, d_model), ln1_b[l].reshape(1, d_model)
        g2, b2 = ln2_g[l].reshape(1, d_model), ln2_b[l].reshape(1, d_model)

        u, q_b, k_f, k_b, v_f, v_b, ga, gb, kmean = _proj(
            yp, w_in_b, cos_p, sin_p, d_pool=d_pool, q_dtype=BF16, q_scale=SCALE * LOG2_E, block_means=True)
        pool = _pool_prompt(u.reshape(batch, seq, d_pool), w_pool_b, ps)
        attn = _moba_prompt(q_b.reshape(batch, seq, D_ATTN), k_b.reshape(batch, seq, D_ATTN),
                            v_b.reshape(batch, seq, D_ATTN), kmean.reshape(batch, seq // BLOCK_SIZE, D_ATTN))
        x1 = _merge(yp, pool.reshape(batch * seq, d_pool), attn.reshape(batch * seq, D_ATTN), ga, gb,
                    wup_b, wua_b, wo_b, g1, b1, alpha)
        yp = _ffn(x1, wg_b, wu_b, wd_b, g2, b2, alpha)
        outs[0].append(k_f.reshape(batch, seq, N_HEADS, HEAD_DIM))
        outs[1].append(v_f.reshape(batch, seq, N_HEADS, HEAD_DIM))
        outs[2].append(u.reshape(batch, seq, d_pool)[:, seq - POOL_STATE_LEN:])

        u_s, q_s, k_s, _, v_s, _, ga_s, gb_s = _proj(
            ys, w_in_b, cos_s, sin_s, d_pool=d_pool, q_dtype=F32, q_scale=1.0, block_means=False)
        u_s3 = u_s.reshape(db, t_new, d_pool)
        pool_s = _pool_sample(state_pool[l].transpose(1, 0, 2), u_s3.transpose(1, 0, 2), w_pool_b, ps)
        pool_s = pool_s.transpose(1, 0, 2)
        q_s3, k_s3, v_s3 = (a.reshape(db, t_new, D_ATTN) for a in (q_s, k_s, v_s))
        kmean_s = _cache_block_mean(cache_k, pt_flat, l, db, n_pages)
        idx = _sample_topk(q_s3, kmean_s)
        attn_s = _sample_attn(q_s3, k_s3, v_s3, cache_k, cache_v, pt_flat, idx.reshape(-1), l, n_pages)
        x1_s = _merge(ys, pool_s.reshape(db * t_new, d_pool), attn_s.reshape(db * t_new, D_ATTN), ga_s, gb_s,
                      wup_b, wua_b, wo_b, g1, b1, alpha)
        ys = _ffn(x1_s, wg_b, wu_b, wd_b, g2, b2, alpha)
        outs[3].append(k_s3.reshape(db, t_new, N_HEADS, HEAD_DIM))
        outs[4].append(v_s3.reshape(db, t_new, N_HEADS, HEAD_DIM))
        outs[5].append(jnp.concatenate([state_pool[l], u_s3], axis=1)[:, t_new:])

    return (yp.reshape(batch, seq, d_model), ys.reshape(db, t_new, d_model),
            jnp.stack(outs[0]), jnp.stack(outs[1]), jnp.stack(outs[2]),
            jnp.stack(outs[3]), jnp.stack(outs[4]), jnp.stack(outs[5]))
```

```python
import functools

import jax
import jax.numpy as jnp
from jax import lax
from jax.experimental import pallas as pl
from jax.experimental.pallas import tpu as pltpu

F32 = jnp.float32
BF16 = jnp.bfloat16

N_HEADS = 8
HEAD_DIM = 128
D_ATTN = N_HEADS * HEAD_DIM
BLOCK_SIZE = 256
TOP_K = 3
PAGE_SIZE = 128
PAGES_PER_BLOCK = BLOCK_SIZE // PAGE_SIZE
ROPE_THETA = 10000.0
POOL_WINDOWS = (2, 4, 8, 16)
POOL_STATE_LEN = max(POOL_WINDOWS) - 1
POOL_HALO = 16
LN_EPS = 1e-5
NEG_INF = -1e30
SCALE = HEAD_DIM ** -0.5
LOG2_E = 1.4426950408889634

ROW_TILE = 256
POOL_TILE = 512
MOBA_PAD = 4
VMEM_LIMIT = 56 * 1024 * 1024

NT_DIMS = (((1,), (1,)), ((), ()))


def _resident(shape):
    return pl.BlockSpec(shape, lambda *_: (0,) * len(shape), pipeline_mode=pl.Buffered(1))


def _layer_norm(z, g, b):
    mu = jnp.mean(z, axis=-1, keepdims=True)
    zc = z - mu
    var = jnp.mean(zc * zc, axis=-1, keepdims=True)
    return zc * lax.rsqrt(var + LN_EPS) * g + b


def _proj_kernel(x_ref, w_ref, cos_ref, sin_ref, u_ref, q_ref, kf_ref, kb_ref, vf_ref, vb_ref,
                 ga_ref, gb_ref, *km_ref, d_pool, q_scale):
    xb = x_ref[...].astype(BF16)
    cos = cos_ref[...]
    sin = sin_ref[...]

    def proj(c0, width):
        return jnp.dot(xb, w_ref[:, c0:c0 + width], preferred_element_type=F32)

    def rope(h):
        return h * cos + pltpu.roll(h, HEAD_DIM // 2, 1) * sin

    u_ref[...] = proj(0, d_pool)
    half = D_ATTN // 2
    for c in range(2):
        acc = proj(d_pool + c * half, half)
        for hh in range(half // HEAD_DIM):
            sl = slice(c * half + hh * HEAD_DIM, c * half + (hh + 1) * HEAD_DIM)
            r = rope(acc[:, hh * HEAD_DIM:(hh + 1) * HEAD_DIM])
            q_ref[:, sl] = (r * q_scale).astype(q_ref.dtype)
    for c in range(2):
        acc = proj(d_pool + D_ATTN + c * half, half)
        for hh in range(half // HEAD_DIM):
            sl = slice(c * half + hh * HEAD_DIM, c * half + (hh + 1) * HEAD_DIM)
            r = rope(acc[:, hh * HEAD_DIM:(hh + 1) * HEAD_DIM])
            kf_ref[:, sl] = r
            kb_ref[:, sl] = r.astype(BF16)
            if km_ref:
                km_ref[0][0, :, sl] = jnp.sum(r, axis=0, keepdims=True) * (1.0 / r.shape[0])
    for c in range(2):
        acc = proj(d_pool + 2 * D_ATTN + c * half, half)
        vf_ref[:, c * half:(c + 1) * half] = acc
        vb_ref[:, c * half:(c + 1) * half] = acc.astype(BF16)
    d_model = ga_ref.shape[1]
    for c in range(2):
        hw = d_model // 2
        ga_ref[:, c * hw:(c + 1) * hw] = proj(d_pool + 3 * D_ATTN + c * hw, hw)
        gb_ref[:, c * hw:(c + 1) * hw] = proj(d_pool + 3 * D_ATTN + d_model + c * hw, hw)


def _proj(x, w_in_b, cos, sin, *, d_pool, q_dtype, q_scale, block_means):
    m, d_model = x.shape
    n_in = w_in_b.shape[1]
    tm = ROW_TILE
    n_tab = cos.shape[0] // tm
    row = lambda i: (i, 0)
    out_shape = [
        jax.ShapeDtypeStruct((m, d_pool), F32),
        jax.ShapeDtypeStruct((m, D_ATTN), q_dtype),
        jax.ShapeDtypeStruct((m, D_ATTN), F32),
        jax.ShapeDtypeStruct((m, D_ATTN), BF16),
        jax.ShapeDtypeStruct((m, D_ATTN), F32),
        jax.ShapeDtypeStruct((m, D_ATTN), BF16),
        jax.ShapeDtypeStruct((m, d_model), F32),
        jax.ShapeDtypeStruct((m, d_model), F32),
    ]
    out_specs = [
        pl.BlockSpec((tm, d_pool), row),
        pl.BlockSpec((tm, D_ATTN), row),
        pl.BlockSpec((tm, D_ATTN), row),
        pl.BlockSpec((tm, D_ATTN), row),
        pl.BlockSpec((tm, D_ATTN), row),
        pl.BlockSpec((tm, D_ATTN), row),
        pl.BlockSpec((tm, d_model), row),
        pl.BlockSpec((tm, d_model), row),
    ]
    if block_means:
        out_shape.append(jax.ShapeDtypeStruct((m // tm, 1, D_ATTN), F32))
        out_specs.append(pl.BlockSpec((1, 1, D_ATTN), lambda i: (i, 0, 0)))
    return pl.pallas_call(
        functools.partial(_proj_kernel, d_pool=d_pool, q_scale=q_scale),
        grid=(m // tm,),
        in_specs=[
            pl.BlockSpec((tm, d_model), row),
            _resident((d_model, n_in)),
            pl.BlockSpec((tm, HEAD_DIM), lambda i: (i % n_tab, 0)),
            pl.BlockSpec((tm, HEAD_DIM), lambda i: (i % n_tab, 0)),
        ],
        out_specs=out_specs,
        out_shape=out_shape,
        compiler_params=pltpu.CompilerParams(dimension_semantics=("parallel",), vmem_limit_bytes=VMEM_LIMIT),
        name="proj_rope",
    )(x, w_in_b, cos, sin)


def _rope_tables(pos):
    half = HEAD_DIM // 2
    inv = ROPE_THETA ** (-jnp.arange(half, dtype=F32) / half)
    ang = pos.astype(F32)[:, None] * inv
    cos, sin = jnp.cos(ang), jnp.sin(ang)
    return jnp.concatenate([cos, cos], -1), jnp.concatenate([-sin, sin], -1)


def _pool_prompt_kernel(u_ref, halo_ref, wp_ref, ps_ref, o_ref, ext_scr):
    i = pl.program_id(1)
    tt = u_ref.shape[1]
    gd = wp_ref.shape[1]
    ext_scr[0:POOL_HALO, :] = jnp.where(i == 0, 0.0, halo_ref[0])
    ext_scr[POOL_HALO:, :] = u_ref[0]
    pos = i * tt + lax.broadcasted_iota(jnp.int32, (tt, 1), 0)
    for g, w in enumerate(POOL_WINDOWS):
        cols = slice(g * gd, (g + 1) * gd)
        win = ext_scr[POOL_HALO:POOL_HALO + tt, cols]
        for back in range(1, w):
            win = win + ext_scr[POOL_HALO - back:POOL_HALO - back + tt, cols]
        cnt = jnp.minimum(pos + 1, w).astype(F32)
        d = win / cnt - ext_scr[POOL_HALO:POOL_HALO + tt, cols]
        y = jnp.dot(d.astype(BF16), wp_ref[g], preferred_element_type=F32) * ps_ref[:, cols]
        o_ref[0, :, cols] = y.astype(o_ref.dtype)


def _pool_prompt(u, w_pool_b, pool_scale):
    b, t, d_pool = u.shape
    tt = POOL_TILE
    halo_blocks = tt // POOL_HALO
    return pl.pallas_call(
        _pool_prompt_kernel,
        grid=(b, t // tt),
        in_specs=[
            pl.BlockSpec((1, tt, d_pool), lambda bi, i: (bi, i, 0)),
            pl.BlockSpec((1, POOL_HALO, d_pool), lambda bi, i: (bi, jnp.maximum(i * halo_blocks - 1, 0), 0)),
            _resident(w_pool_b.shape),
            _resident(pool_scale.shape),
        ],
        out_specs=pl.BlockSpec((1, tt, d_pool), lambda bi, i: (bi, i, 0)),
        out_shape=jax.ShapeDtypeStruct((b, t, d_pool), BF16),
        scratch_shapes=[pltpu.VMEM((POOL_HALO + tt, d_pool), F32)],
        compiler_params=pltpu.CompilerParams(dimension_semantics=("parallel", "parallel")),
        name="pool_prompt",
    )(u, u, w_pool_b, pool_scale)


def _pool_sample_kernel(st_ref, u_ref, wp_ref, ps_ref, o_ref):
    t_new = u_ref.shape[0]
    gd = wp_ref.shape[1]
    p = POOL_STATE_LEN

    def ext_row(j, cols):
        return st_ref[j, :, cols] if j < p else u_ref[j - p, :, cols]

    for g, w in enumerate(POOL_WINDOWS):
        cols = slice(g * gd, (g + 1) * gd)
        for t in range(t_new):
            win = ext_row(p + t, cols)
            for back in range(1, w):
                win = win + ext_row(p + t - back, cols)
            d = win / float(w) - ext_row(p + t, cols)
            y = jnp.dot(d.astype(BF16), wp_ref[g], preferred_element_type=F32) * ps_ref[:, cols]
            o_ref[t, :, cols] = y.astype(o_ref.dtype)


def _pool_sample(state_t, u_t, w_pool_b, pool_scale):
    t_new, db, d_pool = u_t.shape
    return pl.pallas_call(
        _pool_sample_kernel,
        grid=(1,),
        in_specs=[_resident(state_t.shape), _resident(u_t.shape), _resident(w_pool_b.shape),
                  _resident(pool_scale.shape)],
        out_specs=pl.BlockSpec((t_new, db, d_pool), lambda i: (0, 0, 0)),
        out_shape=jax.ShapeDtypeStruct((t_new, db, d_pool), BF16),
        name="pool_sample",
    )(state_t, u_t, w_pool_b, pool_scale)


def _select_blocks(g, n_valid, k_keep):
    nb = g.shape[0]
    n_iota = lax.broadcasted_iota(jnp.int32, g.shape, 0)
    valid = n_iota < n_valid
    g = jnp.where(valid, g, NEG_INF)
    rank = jnp.zeros(g.shape, jnp.int32)
    for m in range(nb):
        row = g[m:m + 1, :]
        ahead = (row > g) | ((row == g) & (n_iota > m))
        rank = rank + ahead.astype(jnp.int32)
    return rank, valid & (rank < k_keep)


def _fold_rows(x, op):
    parts = [x[i:i + 8] for i in range(0, x.shape[0], 8)]
    while len(parts) > 1:
        parts = [op(parts[i], parts[i + 1]) for i in range(0, len(parts), 2)]
    return parts[0]


def _block_means(pages, o_ref):
    for n in range(len(pages) // PAGES_PER_BLOCK):
        tot = jnp.sum(pages[PAGES_PER_BLOCK * n][...], axis=0)
        for i in range(1, PAGES_PER_BLOCK):
            tot = tot + jnp.sum(pages[PAGES_PER_BLOCK * n + i][...], axis=0)
        tot = tot * (1.0 / BLOCK_SIZE)
        for h in range(N_HEADS):
            o_ref[0, n:n + 1, h * HEAD_DIM:(h + 1) * HEAD_DIM] = tot[h:h + 1, :]


def _moba_prompt_kernel(pt_ref, q_ref, k_ref, vt_ref, km_ref, *refs, n_cache_pages):
    pages, (o_ref, cm_ref, sel_scr, s_scr, p_scr) = refs[:n_cache_pages], refs[n_cache_pages:]
    _block_means(pages, cm_ref)
    qi = pl.program_id(2)
    nb = km_ref.shape[1]
    bs = BLOCK_SIZE
    q = q_ref[0]
    km = km_ref[0]
    km_hi = km.astype(BF16)
    km_lo = (km - km_hi.astype(F32)).astype(BF16)
    gate = (lax.dot_general(km_hi, q, NT_DIMS, preferred_element_type=F32)
            + lax.dot_general(km_lo, q, NT_DIMS, preferred_element_type=F32))
    _, sel = _select_blocks(gate, qi, jnp.minimum(TOP_K, qi))
    sel_scr[...] = sel.astype(F32)

    def rows(j):
        return pl.ds(pl.multiple_of(j * bs, bs), bs)

    def own():
        key_i = lax.broadcasted_iota(jnp.int32, (bs, bs), 0)
        qry_i = lax.broadcasted_iota(jnp.int32, (bs, bs), 1)
        return jnp.where(key_i <= qry_i, s_scr[rows(qi), :], NEG_INF)

    def attend(n_pad):
        nk = n_pad * bs
        s_scr[0:nk, :] = lax.dot_general(k_ref[0, 0:nk, :], q, NT_DIMS, preferred_element_type=F32)

        def mask_max(j, m8):
            s = jnp.where(sel_scr[pl.ds(j, 1), :] > 0.5, s_scr[rows(j), :], NEG_INF)
            s_scr[rows(j), :] = s
            return jnp.maximum(m8, _fold_rows(s, jnp.maximum))

        m8 = lax.fori_loop(0, qi, mask_max, _fold_rows(own(), jnp.maximum))
        m = jnp.max(m8, axis=0, keepdims=True)

        def weigh(j, l8):
            p = jnp.exp2(s_scr[rows(j), :] - m)
            p_scr[rows(j), :] = p.astype(BF16)
            return l8 + _fold_rows(p, jnp.add)

        p_own = jnp.exp2(own() - m)
        p_scr[rows(qi), :] = p_own.astype(BF16)
        l8 = lax.fori_loop(0, qi, weigh, _fold_rows(p_own, jnp.add))

        def clear(j, carry):
            p_scr[rows(j), :] = jnp.zeros((bs, bs), BF16)
            return carry

        lax.fori_loop(qi + 1, n_pad, clear, 0)
        l = jnp.sum(l8, axis=0, keepdims=True)
        acc = jnp.dot(vt_ref[0, 0, :, 0:nk], p_scr[0:nk, :], preferred_element_type=F32)
        o_ref[0] = (acc / l).T.astype(o_ref.dtype)

    for n_pad in range(MOBA_PAD, nb + 1, MOBA_PAD):
        pl.when((qi + 1 > n_pad - MOBA_PAD) & (qi + 1 <= n_pad))(functools.partial(attend, n_pad))


def _moba_prompt(q_b, k_b, v_b, kmean, cache, pt_flat, layer, db, n_pages):
    b, t, _ = q_b.shape
    nb = t // BLOCK_SIZE
    assert nb % MOBA_PAD == 0
    n_steps = b * N_HEADS * nb
    pages_per_step = (db * n_pages) // n_steps
    assert pages_per_step * n_steps == db * n_pages and pages_per_step % PAGES_PER_BLOCK == 0
    assert n_pages % pages_per_step == 0
    steps_per_seq = n_pages // pages_per_step
    blocks_per_step = pages_per_step // PAGES_PER_BLOCK

    def step_of(bi, h, i):
        return (bi * N_HEADS + h) * nb + i

    def page_map(bi, h, i, pt, *, j):
        s = step_of(bi, h, i)
        return (layer, pt[(s // steps_per_seq) * n_pages + (s % steps_per_seq) * pages_per_step + j], 0, 0, 0)

    def mean_map(bi, h, i, pt):
        s = step_of(bi, h, i)
        return (s // steps_per_seq, s % steps_per_seq, 0)

    v_t = v_b.reshape(b, t, N_HEADS, HEAD_DIM).transpose(0, 2, 3, 1)
    return pl.pallas_call(
        functools.partial(_moba_prompt_kernel, n_cache_pages=pages_per_step),
        grid_spec=pltpu.PrefetchScalarGridSpec(
            num_scalar_prefetch=1,
            grid=(b, N_HEADS, nb),
            in_specs=[
                pl.BlockSpec((1, BLOCK_SIZE, HEAD_DIM), lambda bi, h, i, pt: (bi, i, h)),
                pl.BlockSpec((1, t, HEAD_DIM), lambda bi, h, i, pt: (bi, 0, h)),
                pl.BlockSpec((1, 1, HEAD_DIM, t), lambda bi, h, i, pt: (bi, h, 0, 0)),
                pl.BlockSpec((1, nb, HEAD_DIM), lambda bi, h, i, pt: (bi, 0, h)),
            ] + [pl.BlockSpec((None, None, PAGE_SIZE, N_HEADS, HEAD_DIM), functools.partial(page_map, j=j))
                 for j in range(pages_per_step)],
            out_specs=[pl.BlockSpec((1, BLOCK_SIZE, HEAD_DIM), lambda bi, h, i, pt: (bi, i, h)),
                       pl.BlockSpec((1, blocks_per_step, D_ATTN), mean_map)],
            scratch_shapes=[pltpu.VMEM((nb, BLOCK_SIZE), F32), pltpu.VMEM((t, BLOCK_SIZE), F32),
                            pltpu.VMEM((t, BLOCK_SIZE), BF16)],
        ),
        out_shape=[jax.ShapeDtypeStruct((b, t, D_ATTN), BF16),
                   jax.ShapeDtypeStruct((db, n_pages // PAGES_PER_BLOCK, D_ATTN), F32)],
        compiler_params=pltpu.CompilerParams(dimension_semantics=("arbitrary", "arbitrary", "arbitrary"),
                                             vmem_limit_bytes=VMEM_LIMIT),
        name="moba_prompt",
    )(pt_flat, q_b, k_b, v_t, kmean, *([cache] * pages_per_step))


def _sample_topk_kernel(q_ref, km_ref, idx_ref):
    q = q_ref[0]
    km = km_ref[0]
    nb = km.shape[0]
    lane = lax.broadcasted_iota(jnp.int32, (nb, 128), 1)
    blk = lax.broadcasted_iota(jnp.int32, (nb, 128), 0).astype(F32)
    idx_ref[...] = jnp.zeros(idx_ref.shape, jnp.int32)
    for t in range(q.shape[0]):
        prod = km * q[t:t + 1, :]
        gate = jnp.zeros((nb, 128), F32)
        for h in range(N_HEADS):
            s = jnp.sum(prod[:, h * HEAD_DIM:(h + 1) * HEAD_DIM], axis=1, keepdims=True)
            gate = jnp.where(lane == h, s, gate)
        rank, _ = _select_blocks(gate, nb, TOP_K)
        for k in range(TOP_K):
            idx_ref[0, TOP_K * t + k:TOP_K * t + k + 1, :] = jnp.sum(
                jnp.where(rank == k, blk, 0.0), axis=0, keepdims=True).astype(jnp.int32)


def _sample_topk(q_s, kmean):
    db, t_new, _ = q_s.shape
    nb = kmean.shape[1]
    rows = -(-t_new * TOP_K // 8) * 8
    out = pl.pallas_call(
        _sample_topk_kernel,
        grid=(db,),
        in_specs=[pl.BlockSpec((1, t_new, D_ATTN), lambda bi: (bi, 0, 0)),
                  pl.BlockSpec((1, nb, D_ATTN), lambda bi: (bi, 0, 0))],
        out_specs=pl.BlockSpec((1, rows, 128), lambda bi: (bi, 0, 0)),
        out_shape=jax.ShapeDtypeStruct((db, rows, 128), jnp.int32),
        compiler_params=pltpu.CompilerParams(dimension_semantics=("parallel",)),
        name="sample_topk",
    )(q_s, kmean)
    return out[:, :t_new * TOP_K, :N_HEADS].reshape(db, t_new, TOP_K, N_HEADS)


def _sample_attn_kernel(pt_ref, idx_ref, q_ref, kn_ref, vn_ref, ck_hbm, cv_hbm, o_ref, kbuf, vbuf, sem,
                        *, t_new, n_pages, layer):
    n_sel = TOP_K * PAGES_PER_BLOCK
    step = pl.program_id(0) * N_HEADS + pl.program_id(1)
    n_steps = pl.num_programs(0) * N_HEADS
    slot = step % 2

    def page_copies(page, h, slot_, j):
        return (pltpu.make_async_copy(ck_hbm.at[layer, page, :, h, :], kbuf.at[slot_, j], sem.at[0, slot_]),
                pltpu.make_async_copy(cv_hbm.at[layer, page, :, h, :], vbuf.at[slot_, j], sem.at[1, slot_]))

    def start_gather(step_, slot_):
        b_ = step_ // N_HEADS
        h_ = step_ % N_HEADS
        for t in range(t_new):
            for k in range(TOP_K):
                blk = idx_ref[((b_ * t_new + t) * TOP_K + k) * N_HEADS + h_]
                for i in range(PAGES_PER_BLOCK):
                    page = pt_ref[b_ * n_pages + blk * PAGES_PER_BLOCK + i]
                    for c in page_copies(page, h_, slot_, (t * TOP_K + k) * PAGES_PER_BLOCK + i):
                        c.start()

    @pl.when(step == 0)
    def _():
        start_gather(step, slot)

    @pl.when(step + 1 < n_steps)
    def _():
        start_gather(step + 1, 1 - slot)

    for j in range(t_new * n_sel):
        for c in page_copies(0, 0, slot, j):
            c.wait()

    q = q_ref[0]
    kn = kn_ref[0]
    vn = vn_ref[0]
    own_i = lax.broadcasted_iota(jnp.int32, (t_new, 1), 0)
    for t in range(t_new):
        qt = q[t:t + 1, :]
        s_sel = [jnp.sum(kbuf[slot, t * n_sel + p] * qt, axis=1, keepdims=True) * SCALE for p in range(n_sel)]
        s_own = jnp.sum(kn * qt, axis=1, keepdims=True) * SCALE
        s_own = jnp.where(own_i <= t, s_own, NEG_INF)
        m = jnp.max(s_own, axis=0, keepdims=True)
        for s in s_sel:
            m = jnp.maximum(m, jnp.max(s, axis=0, keepdims=True))
        p_own = jnp.exp(s_own - m)
        l = jnp.sum(p_own, axis=0, keepdims=True)
        o = jnp.sum(p_own * vn, axis=0, keepdims=True)
        for p in range(n_sel):
            pp = jnp.exp(s_sel[p] - m)
            l = l + jnp.sum(pp, axis=0, keepdims=True)
            o = o + jnp.sum(pp * vbuf[slot, t * n_sel + p], axis=0, keepdims=True)
        o_ref[0, t:t + 1, :] = o / l


def _sample_attn(q_s, k_s, v_s, cache_k, cache_v, pt_flat, idx_flat, layer, n_pages):
    db, t_new, _ = q_s.shape
    n_slices = t_new * TOP_K * PAGES_PER_BLOCK
    new_spec = pl.BlockSpec((1, t_new, HEAD_DIM), lambda bi, h, pt, idx: (bi, 0, h))
    return pl.pallas_call(
        functools.partial(_sample_attn_kernel, t_new=t_new, n_pages=n_pages, layer=layer),
        grid_spec=pltpu.PrefetchScalarGridSpec(
            num_scalar_prefetch=2,
            grid=(db, N_HEADS),
            in_specs=[new_spec, new_spec, new_spec, pl.BlockSpec(memory_space=pl.ANY),
                      pl.BlockSpec(memory_space=pl.ANY)],
            out_specs=new_spec,
            scratch_shapes=[pltpu.VMEM((2, n_slices, PAGE_SIZE, HEAD_DIM), F32),
                            pltpu.VMEM((2, n_slices, PAGE_SIZE, HEAD_DIM), F32),
                            pltpu.SemaphoreType.DMA((2, 2))],
        ),
        out_shape=jax.ShapeDtypeStruct((db, t_new, D_ATTN), F32),
        compiler_params=pltpu.CompilerParams(dimension_semantics=("arbitrary", "arbitrary")),
        name="sample_attn",
    )(pt_flat, idx_flat, q_s, k_s, v_s, cache_k, cache_v)


def _merge_kernel(x_ref, pool_ref, attn_ref, ga_ref, gb_ref, wup_ref, wua_ref, wo_ref, g_ref, b_ref, o_ref,
                  *, alpha):
    pu = jnp.dot(pool_ref[...], wup_ref[...], preferred_element_type=F32)
    au = jnp.dot(attn_ref[...].astype(BF16), wua_ref[...], preferred_element_type=F32)
    mix = jax.nn.sigmoid(ga_ref[...]) * pu + jax.nn.sigmoid(gb_ref[...]) * au
    z = alpha * x_ref[...] + jnp.dot(mix.astype(BF16), wo_ref[...], preferred_element_type=F32)
    o_ref[...] = _layer_norm(z, g_ref[...], b_ref[...])


def _merge(x, pool, attn, ga, gb, wup_b, wua_b, wo_b, ln_g, ln_b, alpha):
    m, d_model = x.shape
    tm = ROW_TILE
    row = lambda i: (i, 0)
    return pl.pallas_call(
        functools.partial(_merge_kernel, alpha=alpha),
        grid=(m // tm,),
        in_specs=[pl.BlockSpec((tm, d_model), row), pl.BlockSpec((tm, pool.shape[1]), row),
                  pl.BlockSpec((tm, attn.shape[1]), row), pl.BlockSpec((tm, d_model), row),
                  pl.BlockSpec((tm, d_model), row), _resident(wup_b.shape), _resident(wua_b.shape),
                  _resident(wo_b.shape), _resident(ln_g.shape), _resident(ln_b.shape)],
        out_specs=pl.BlockSpec((tm, d_model), row),
        out_shape=jax.ShapeDtypeStruct((m, d_model), F32),
        compiler_params=pltpu.CompilerParams(dimension_semantics=("parallel",), vmem_limit_bytes=VMEM_LIMIT),
        name="merge_ln",
    )(x, pool, attn, ga, gb, wup_b, wua_b, wo_b, ln_g, ln_b)


def _ffn_kernel(x_ref, wg_ref, wu_ref, wd_ref, g_ref, b_ref, o_ref, *, alpha):
    x = x_ref[...]
    xb = x.astype(BF16)
    gate = jnp.dot(xb, wg_ref[...], preferred_element_type=F32)
    up = jnp.dot(xb, wu_ref[...], preferred_element_type=F32)
    h = (jax.nn.silu(gate) * up).astype(BF16)
    z = alpha * x + jnp.dot(h, wd_ref[...], preferred_element_type=F32)
    o_ref[...] = _layer_norm(z, g_ref[...], b_ref[...])


def _ffn(x, wg_b, wu_b, wd_b, ln_g, ln_b, alpha):
    m, d_model = x.shape
    tm = ROW_TILE
    row = lambda i: (i, 0)
    return pl.pallas_call(
        functools.partial(_ffn_kernel, alpha=alpha),
        grid=(m // tm,),
        in_specs=[pl.BlockSpec((tm, d_model), row), _resident(wg_b.shape), _resident(wu_b.shape),
                  _resident(wd_b.shape), _resident(ln_g.shape), _resident(ln_b.shape)],
        out_specs=pl.BlockSpec((tm, d_model), row),
        out_shape=jax.ShapeDtypeStruct((m, d_model), F32),
        compiler_params=pltpu.CompilerParams(dimension_semantics=("parallel",), vmem_limit_bytes=VMEM_LIMIT),
        name="ffn_ln",
    )(x, wg_b, wu_b, wd_b, ln_g, ln_b)


def kernel(x_prompt, x_sample, cache_k, cache_v, state_pool, page_table, w_in, w_pool, pool_scale, w_up_pool,
           w_up_attn, w_out, ln1_g, ln1_b, w_gate, w_up, w_down, ln2_g, ln2_b):
    depth = w_in.shape[0]
    batch, seq, d_model = x_prompt.shape
    db, t_new, _ = x_sample.shape
    d_pool = w_pool.shape[1] * w_pool.shape[2]
    n_pages = page_table.shape[1]
    past_len = n_pages * PAGE_SIZE
    assert seq % BLOCK_SIZE == 0 and past_len % BLOCK_SIZE == 0 and n_pages // PAGES_PER_BLOCK >= TOP_K
    assert seq >= POOL_STATE_LEN and past_len >= POOL_STATE_LEN and t_new <= POOL_STATE_LEN
    alpha = (2.0 * depth) ** 0.25

    cos_p, sin_p = _rope_tables(jnp.arange(seq))
    cos_s, sin_s = _rope_tables(past_len + jnp.arange(db * t_new) % t_new)
    pt_flat = page_table.reshape(-1)

    yp = x_prompt.reshape(batch * seq, d_model)
    ys = x_sample.reshape(db * t_new, d_model)
    outs = [[] for _ in range(6)]
    for l in range(depth):
        w_in_b = w_in[l].astype(BF16)
        w_pool_b = w_pool[l].astype(BF16)
        ps = pool_scale[l].reshape(1, d_pool)
        wup_b, wua_b, wo_b = w_up_pool[l].astype(BF16), w_up_attn[l].astype(BF16), w_out[l].astype(BF16)
        wg_b, wu_b, wd_b = w_gate[l].astype(BF16), w_up[l].astype(BF16), w_down[l].astype(BF16)
        g1, b1 = ln1_g[l].reshape(1, d_model), ln1_b[l].reshape(1, d_model)
        g2, b2 = ln2_g[l].reshape(1, d_model), ln2_b[l].reshape(1, d_model)

        u, q_b, k_f, k_b, v_f, v_b, ga, gb, kmean = _proj(
            yp, w_in_b, cos_p, sin_p, d_pool=d_pool, q_dtype=BF16, q_scale=SCALE * LOG2_E, block_means=True)
        pool = _pool_prompt(u.reshape(batch, seq, d_pool), w_pool_b, ps)
        attn, kmean_s = _moba_prompt(
            q_b.reshape(batch, seq, D_ATTN), k_b.reshape(batch, seq, D_ATTN), v_b.reshape(batch, seq, D_ATTN),
            kmean.reshape(batch, seq // BLOCK_SIZE, D_ATTN), cache_k, pt_flat, l, db, n_pages)
        x1 = _merge(yp, pool.reshape(batch * seq, d_pool), attn.reshape(batch * seq, D_ATTN), ga, gb,
                    wup_b, wua_b, wo_b, g1, b1, alpha)
        yp = _ffn(x1, wg_b, wu_b, wd_b, g2, b2, alpha)
        outs[0].append(k_f.reshape(batch, seq, N_HEADS, HEAD_DIM))
        outs[1].append(v_f.reshape(batch, seq, N_HEADS, HEAD_DIM))
        outs[2].append(u.reshape(batch, seq, d_pool)[:, seq - POOL_STATE_LEN:])

        u_s, q_s, k_s, _, v_s, _, ga_s, gb_s = _proj(
            ys, w_in_b, cos_s, sin_s, d_pool=d_pool, q_dtype=F32, q_scale=1.0, block_means=False)
        u_s3 = u_s.reshape(db, t_new, d_pool)
        pool_s = _pool_sample(state_pool[l].transpose(1, 0, 2), u_s3.transpose(1, 0, 2), w_pool_b, ps)
        pool_s = pool_s.transpose(1, 0, 2)
        q_s3, k_s3, v_s3 = (a.reshape(db, t_new, D_ATTN) for a in (q_s, k_s, v_s))
        idx = _sample_topk(q_s3, kmean_s)
        attn_s = _sample_attn(q_s3, k_s3, v_s3, cache_k, cache_v, pt_flat, idx.reshape(-1), l, n_pages)
        x1_s = _merge(ys, pool_s.reshape(db * t_new, d_pool), attn_s.reshape(db * t_new, D_ATTN), ga_s, gb_s,
                      wup_b, wua_b, wo_b, g1, b1, alpha)
        ys = _ffn(x1_s, wg_b, wu_b, wd_b, g2, b2, alpha)
        outs[3].append(k_s3.reshape(db, t_new, N_HEADS, HEAD_DIM))
        outs[4].append(v_s3.reshape(db, t_new, N_HEADS, HEAD_DIM))
        outs[5].append(jnp.concatenate([state_pool[l], u_s3], axis=1)[:, t_new:])

    return (yp.reshape(batch, seq, d_model), ys.reshape(db, t_new, d_model),
            jnp.stack(outs[0]), jnp.stack(outs[1]), jnp.stack(outs[2]),
            jnp.stack(outs[3]), jnp.stack(outs[4]), jnp.stack(outs[5]))
```
